```python
import math
import jax, jax.numpy as jnp
from jax import lax
import numpy as np

D_MODEL = 1024
BATCH = 16
SEQ = 2048
DEPTH = 2

N_MIXERS = 2
N_A_LAYERS = (DEPTH + N_MIXERS - 1) // N_MIXERS
N_B_LAYERS = DEPTH // N_MIXERS

ATTN_HEAD_DIM = 128
ATTN_HEADS = D_MODEL // ATTN_HEAD_DIM
MOBA_BLOCK = 256
MOBA_TOPK = 3
Q_CHUNK = 128
ROPE_THETA = 10000.0

SSM_EXPAND = 2
SSM_D_INNER = SSM_EXPAND * D_MODEL
SSM_HEAD_DIM = 64
SSM_HEADS = SSM_D_INNER // SSM_HEAD_DIM
SSM_GROUPS = 8
SSM_HEADS_PER_GROUP = SSM_HEADS // SSM_GROUPS
SSM_STATE = 128
SSM_CONV = 4
SSM_CHUNK = 128
SSM_CONV_DIM = SSM_D_INNER + 2 * SSM_GROUPS * SSM_STATE
SSM_IN_DIM = SSM_D_INNER + SSM_CONV_DIM + SSM_HEADS

D_FF = 2816
EPS = 1e-6

kernel_name = "moba_mamba2_macaron_hybrid"


def rms_norm(x, g):
    xf = x.astype(jnp.float32)
    y = xf * lax.rsqrt(jnp.mean(xf * xf, axis=-1, keepdims=True) + EPS)
    return (y * g.astype(jnp.float32)).astype(x.dtype)


def swiglu(x, w_gate, w_up, w_down):
    return (jax.nn.silu(x @ w_gate) * (x @ w_up)) @ w_down


def rope(x, positions):
    half = x.shape[-1] // 2
    inv_freq = ROPE_THETA ** (-jnp.arange(half, dtype=jnp.float32) / half)
    ang = positions.astype(jnp.float32)[..., None] * inv_freq
    cos = jnp.cos(ang)[:, :, None, :]
    sin = jnp.sin(ang)[:, :, None, :]
    xf = x.astype(jnp.float32)
    x1, x2 = xf[..., :half], xf[..., half:]
    out = jnp.concatenate([x1 * cos - x2 * sin, x2 * cos + x1 * sin], axis=-1)
    return out.astype(x.dtype)


def moba_mixer(h, positions, w_qkv, q_gain, k_gain, w_o):
    bsz, seq, _ = h.shape
    H, Dh, BS = ATTN_HEADS, ATTN_HEAD_DIM, MOBA_BLOCK
    qkv = (h @ w_qkv).reshape(bsz, seq, 3, H, Dh)
    q = rope(rms_norm(qkv[:, :, 0], q_gain), positions)
    k = rope(rms_norm(qkv[:, :, 1], k_gain), positions)
    v = qkv[:, :, 2]
    q = jnp.swapaxes(q, 1, 2)
    k = jnp.swapaxes(k, 1, 2)
    v = jnp.swapaxes(v, 1, 2)
    n_blk = -(-seq // BS)
    pad = n_blk * BS - seq
    k = jnp.pad(k, ((0, 0), (0, 0), (0, pad), (0, 0)))
    v = jnp.pad(v, ((0, 0), (0, 0), (0, pad), (0, 0)))
    kb = k.reshape(bsz, H, n_blk, BS, Dh)
    vb = v.reshape(bsz, H, n_blk, BS, Dh)
    n_sel = min(MOBA_TOPK, n_blk - 1)
    q_blk = jnp.arange(seq) // BS
    scale = Dh ** -0.5

    if n_sel > 0:
        k_mean = jnp.mean(kb.astype(jnp.float32), axis=3)
        gate = jnp.einsum('bhtd,bhnd->bhtn', q.astype(jnp.float32), k_mean)
        past = jnp.arange(n_blk)[None, :] < q_blk[:, None]
        gate = jnp.where(past, gate, -jnp.inf)
        _, sel_idx = lax.top_k(gate, n_sel)
        sel_valid = sel_idx < q_blk[:, None]

    n_qc = seq // Q_CHUNK
    head_ix = jnp.arange(H)[:, None, None]

    def one_chunk(s):
        b = s // n_qc
        t0 = (s % n_qc) * Q_CHUNK
        qc = lax.dynamic_slice_in_dim(lax.dynamic_index_in_dim(q, b, 0, False), t0, Q_CHUNK, 1)
        kb_b = lax.dynamic_index_in_dim(kb, b, 0, False)
        vb_b = lax.dynamic_index_in_dim(vb, b, 0, False)
        own = t0 // BS
        k_own = lax.dynamic_index_in_dim(kb_b, own, 1, False)
        v_own = lax.dynamic_index_in_dim(vb_b, own, 1, False)
        q_pos = t0 + jnp.arange(Q_CHUNK)
        k_pos = own * BS + jnp.arange(BS)
        s_own = jnp.einsum('hqd,hkd->hqk', qc, k_own).astype(jnp.float32) * scale
        s_own = jnp.where(k_pos[None, :] <= q_pos[:, None], s_own, -jnp.inf)
        if n_sel == 0:
            p = jax.nn.softmax(s_own, axis=-1).astype(v_own.dtype)
            return jnp.einsum('hqk,hkd->hqd', p, v_own)
        idx = lax.dynamic_slice_in_dim(lax.dynamic_index_in_dim(sel_idx, b, 0, False), t0, Q_CHUNK, 1)
        valid = lax.dynamic_slice_in_dim(lax.dynamic_index_in_dim(sel_valid, b, 0, False), t0, Q_CHUNK, 1)
        k_sel = kb_b[head_ix, idx]
        v_sel = vb_b[head_ix, idx]
        s_sel = jnp.einsum('hqd,hqjkd->hqjk', qc, k_sel).astype(jnp.float32) * scale
        s_sel = jnp.where(valid[..., None], s_sel, -jnp.inf).reshape(H, Q_CHUNK, n_sel * BS)
        p = jax.nn.softmax(jnp.concatenate([s_sel, s_own], axis=-1), axis=-1).astype(v_own.dtype)
        p_sel = p[..., :n_sel * BS].reshape(H, Q_CHUNK, n_sel, BS)
        p_own = p[..., n_sel * BS:]
        return (jnp.einsum('hqjk,hqjkd->hqd', p_sel, v_sel)
                + jnp.einsum('hqk,hkd->hqd', p_own, v_own))

    o = lax.map(one_chunk, jnp.arange(bsz * n_qc))
    o = o.reshape(bsz, n_qc, H, Q_CHUNK, Dh).transpose(0, 1, 3, 2, 4).reshape(bsz, seq, H * Dh)
    return o @ w_o


def ssd_chunked(xdt, a_dt, bm, cm):
    bsz, seq = xdt.shape[:2]
    G, R, P, N, CH = SSM_GROUPS, SSM_HEADS_PER_GROUP, SSM_HEAD_DIM, SSM_STATE, SSM_CHUNK
    nc = seq // CH
    x = xdt.reshape(bsz, nc, CH, G, R, P)
    a = a_dt.reshape(bsz, nc, CH, G, R).transpose(0, 3, 4, 1, 2)
    bc = bm.reshape(bsz, nc, CH, G, N)
    cc = cm.reshape(bsz, nc, CH, G, N)
    a_cs = jnp.cumsum(a, axis=-1)
    tril = jnp.tril(jnp.ones((CH, CH), dtype=bool))
    seg = a_cs[..., :, None] - a_cs[..., None, :]
    L = jnp.exp(jnp.where(tril, seg, -jnp.inf))
    cb = jnp.einsum('bclgn,bcsgn->bgcls', cc, bc)
    y_diag = jnp.einsum('bgrcls,bcsgrp->bclgrp', cb[:, :, None] * L, x)
    decay = jnp.exp(a_cs[..., -1:] - a_cs)
    states = jnp.einsum('bclgn,bgrcl,bclgrp->bcgrpn', bc, decay, x)
    chunk_decay = jnp.exp(a_cs[..., -1])

    def step(hs, inp):
        st, dec = inp
        return hs * dec[..., None, None] + st, hs

    h0 = jnp.zeros((bsz, G, R, P, N), dtype=jnp.float32)
    _, prev = lax.scan(step, h0, (jnp.moveaxis(states, 1, 0), jnp.moveaxis(chunk_decay, -1, 0)))
    prev = jnp.moveaxis(prev, 0, 1)
    y_off = jnp.einsum('bclgn,bcgrpn,bgrcl->bclgrp', cc, prev, jnp.exp(a_cs))
    return (y_diag + y_off).reshape(bsz, seq, G * R, P)


def mamba2_mixer(h, w_in, conv_w, conv_b, dt_bias, a_log, d_skip, norm_g, w_out):
    bsz, seq, _ = h.shape
    DI, G, N, H, P = SSM_D_INNER, SSM_GROUPS, SSM_STATE, SSM_HEADS, SSM_HEAD_DIM
    proj = h @ w_in
    z = proj[..., :DI]
    xbc = proj[..., DI:DI + SSM_CONV_DIM]
    dt = proj[..., DI + SSM_CONV_DIM:]
    xbc = lax.conv_general_dilated(xbc, conv_w[:, None, :], window_strides=(1,),
                                   padding=[(SSM_CONV - 1, 0)],
                                   dimension_numbers=('NWC', 'WIO', 'NWC'),
                                   feature_group_count=SSM_CONV_DIM)
    xbc = jax.nn.silu(xbc + conv_b)
    xs = xbc[..., :DI].reshape(bsz, seq, H, P).astype(jnp.float32)
    bm = xbc[..., DI:DI + G * N].reshape(bsz, seq, G, N).astype(jnp.float32)
    cm = xbc[..., DI + G * N:].reshape(bsz, seq, G, N).astype(jnp.float32)
    dt = jax.nn.softplus(dt.astype(jnp.float32) + dt_bias.astype(jnp.float32))
    a = -jnp.exp(a_log.astype(jnp.float32))
    y = ssd_chunked(xs * dt[..., None], dt * a, bm, cm)
    y = y + d_skip.astype(jnp.float32)[:, None] * xs
    y = y.reshape(bsz, seq, DI) * jax.nn.silu(z.astype(jnp.float32))
    yg = y.reshape(bsz, seq, G, DI // G)
    yg = yg * lax.rsqrt(jnp.mean(yg * yg, axis=-1, keepdims=True) + EPS)
    y = yg.reshape(bsz, seq, DI) * norm_g.astype(jnp.float32)
    return y.astype(h.dtype) @ w_out


def setup_inputs(seed: int = 0) -> dict:
    key = jax.random.key(seed)
    ks = jax.random.split(key, 26)
    f32 = jnp.float32

    def nrm(k, shape, fan_in):
        return jax.random.normal(k, shape, f32) * (fan_in ** -0.5)

    def gain(k, shape):
        return 1.0 + 0.05 * jax.random.normal(k, shape, f32)

    dt_init = jnp.exp(jax.random.uniform(ks[20], (N_B_LAYERS, SSM_HEADS), f32)
                      * (math.log(0.1) - math.log(0.001)) + math.log(0.001))
    dt_init = jnp.maximum(dt_init, 1e-4)
    return {
        "x": jax.random.normal(ks[0], (BATCH, SEQ, D_MODEL), f32),
        "positions": jnp.broadcast_to(jnp.arange(SEQ, dtype=jnp.int32), (BATCH, SEQ)),
        "ffn1_norm": gain(ks[1], (DEPTH, D_MODEL)),
        "ffn1_w_gate": nrm(ks[2], (DEPTH, D_MODEL, D_FF), D_MODEL),
        "ffn1_w_up": nrm(ks[3], (DEPTH, D_MODEL, D_FF), D_MODEL),
        "ffn1_w_down": nrm(ks[4], (DEPTH, D_FF, D_MODEL), D_FF),
        "mix_norm": gain(ks[5], (DEPTH, D_MODEL)),
        "ffn2_norm": gain(ks[6], (DEPTH, D_MODEL)),
        "ffn2_w_gate": nrm(ks[7], (DEPTH, D_MODEL, D_FF), D_MODEL),
        "ffn2_w_up": nrm(ks[8], (DEPTH, D_MODEL, D_FF), D_MODEL),
        "ffn2_w_down": nrm(ks[9], (DEPTH, D_FF, D_MODEL), D_FF),
        "attn_w_qkv": nrm(ks[10], (N_A_LAYERS, D_MODEL, 3 * D_MODEL), D_MODEL),
        "attn_q_norm": gain(ks[11], (N_A_LAYERS, ATTN_HEAD_DIM)),
        "attn_k_norm": gain(ks[12], (N_A_LAYERS, ATTN_HEAD_DIM)),
        "attn_w_o": nrm(ks[13], (N_A_LAYERS, D_MODEL, D_MODEL), D_MODEL),
        "ssm_w_in": nrm(ks[14], (N_B_LAYERS, D_MODEL, SSM_IN_DIM), D_MODEL),
        "ssm_conv_w": nrm(ks[15], (N_B_LAYERS, SSM_CONV, SSM_CONV_DIM), SSM_CONV),
        "ssm_conv_b": 0.02 * jax.random.normal(ks[16], (N_B_LAYERS, SSM_CONV_DIM), f32),
        "ssm_dt_bias": dt_init + jnp.log(-jnp.expm1(-dt_init)),
        "ssm_a_log": jnp.log(jax.random.uniform(ks[17], (N_B_LAYERS, SSM_HEADS), f32, 1.0, 16.0)),
        "ssm_d": 1.0 + 0.1 * jax.random.normal(ks[18], (N_B_LAYERS, SSM_HEADS), f32),
        "ssm_norm": gain(ks[19], (N_B_LAYERS, SSM_D_INNER)),
        "ssm_w_out": nrm(ks[21], (N_B_LAYERS, SSM_D_INNER, D_MODEL), SSM_D_INNER),
    }


def reference(x, positions, ffn1_norm, ffn1_w_gate, ffn1_w_up, ffn1_w_down, mix_norm,
              ffn2_norm, ffn2_w_gate, ffn2_w_up, ffn2_w_down,
              attn_w_qkv, attn_q_norm, attn_k_norm, attn_w_o,
              ssm_w_in, ssm_conv_w, ssm_conv_b, ssm_dt_bias, ssm_a_log, ssm_d, ssm_norm, ssm_w_out):
    for i in range(DEPTH):
        x = x + 0.5 * swiglu(rms_norm(x, ffn1_norm[i]), ffn1_w_gate[i], ffn1_w_up[i], ffn1_w_down[i])
        h = rms_norm(x, mix_norm[i])
        j = i // N_MIXERS
        if i % N_MIXERS == 0:
            x = x + moba_mixer(h, positions, attn_w_qkv[j], attn_q_norm[j], attn_k_norm[j], attn_w_o[j])
        else:
            x = x + mamba2_mixer(h, ssm_w_in[j], ssm_conv_w[j], ssm_conv_b[j], ssm_dt_bias[j],
                                 ssm_a_log[j], ssm_d[j], ssm_norm[j], ssm_w_out[j])
        x = x + 0.5 * swiglu(rms_norm(x, ffn2_norm[i]), ffn2_w_gate[i], ffn2_w_up[i], ffn2_w_down[i])
    return x
```

```python
import functools
import math

import numpy as np
import jax
import jax.numpy as jnp
from jax import lax
from jax.experimental import pallas as pl
from jax.experimental.pallas import tpu as pltpu

F32 = jnp.float32
BF16 = jnp.bfloat16

D_MODEL = 1024
D_FF = 2816
EPS = 1e-6

ATTN_HEADS = 8
ATTN_HEAD_DIM = 128
MOBA_BLOCK = 256
MOBA_TOPK = 3
ROPE_THETA = 10000.0

SSM_D_INNER = 2048
SSM_HEAD_DIM = 64
SSM_HEADS = 32
SSM_GROUPS = 8
SSM_STATE = 128
SSM_CONV = 4
SSM_CHUNK = 128
SSM_BC_DIM = SSM_GROUPS * SSM_STATE
SSM_CONV_DIM = SSM_D_INNER + 2 * SSM_BC_DIM

LANES = 128
VMEM_LIMIT_BYTES = 56 * 1024 * 1024

FFN_ROWS = 512
FFN_COLS = 256
PROJ_ROWS = 512
SSM_IN_ROWS = 256
HALO_ROWS = 8
MASK_BIAS = -1e30


def _resident(shape):
    nd = len(shape)
    return pl.BlockSpec(shape, lambda *_: (0,) * nd, pipeline_mode=pl.Buffered(1))


def _params(n_axes):
    return pltpu.CompilerParams(dimension_semantics=("arbitrary",) * n_axes,
                                vmem_limit_bytes=VMEM_LIMIT_BYTES)


def _rms(x, g):
    ms = jnp.mean(x * x, axis=-1, keepdims=True)
    return x * lax.rsqrt(ms + EPS) * g


def _silu(x):
    return x * jax.nn.sigmoid(x)


def _dot(a, b):
    return jnp.dot(a, b, preferred_element_type=F32)


def _dot_nt(a, b):
    return lax.dot_general(a, b, (((1,), (1,)), ((), ())), preferred_element_type=F32)


def _split3(v):
    hi = v.astype(BF16)
    r1 = v - hi.astype(F32)
    mid = r1.astype(BF16)
    lo = (r1 - mid.astype(F32)).astype(BF16)
    return hi, mid, lo


def _ffn_kernel(x_ref, g_ref, wg_ref, wu_ref, wd_ref, o_ref):
    x = x_ref[...]
    n = _rms(x, g_ref[...]).astype(BF16)
    acc = jnp.zeros(x.shape, F32)
    for c in range(D_FF // FFN_COLS):
        sl = slice(c * FFN_COLS, (c + 1) * FFN_COLS)
        gate = _dot(n, wg_ref[:, sl])
        up = _dot(n, wu_ref[:, sl])
        h = (_silu(gate) * up).astype(BF16)
        acc = acc + _dot(h, wd_ref[sl, :])
    o_ref[...] = x + 0.5 * acc


def _ffn(x2d, g, wg, wu, wd):
    m = x2d.shape[0]
    row = pl.BlockSpec((FFN_ROWS, D_MODEL), lambda i: (i, 0))
    return pl.pallas_call(
        _ffn_kernel,
        grid=(m // FFN_ROWS,),
        in_specs=[row, _resident((1, D_MODEL)), _resident((D_MODEL, D_FF)),
                  _resident((D_MODEL, D_FF)), _resident((D_FF, D_MODEL))],
        out_specs=row,
        out_shape=jax.ShapeDtypeStruct(x2d.shape, F32),
        compiler_params=_params(1),
        name="ffn",
    )(x2d, g, wg, wu, wd)


def _proj_res_kernel(x_ref, a_ref, w_ref, o_ref):
    o_ref[...] = x_ref[...] + _dot(a_ref[...], w_ref[...])


def _proj_res(x2d, a2d, w):
    m, k = a2d.shape
    return pl.pallas_call(
        _proj_res_kernel,
        grid=(m // PROJ_ROWS,),
        in_specs=[pl.BlockSpec((PROJ_ROWS, D_MODEL), lambda i: (i, 0)),
                  pl.BlockSpec((PROJ_ROWS, k), lambda i: (i, 0)),
                  _resident((k, D_MODEL))],
        out_specs=pl.BlockSpec((PROJ_ROWS, D_MODEL), lambda i: (i, 0)),
        out_shape=jax.ShapeDtypeStruct(x2d.shape, F32),
        compiler_params=_params(1),
        name="proj_res",
    )(x2d, a2d, w)


def _qkv_kernel(x_ref, g_ref, w_ref, qg_ref, kg_ref, pos_ref, invf_ref, q_ref, k_ref, v_ref):
    n = _rms(x_ref[0], g_ref[...]).astype(BF16)
    qkv = _dot(n, w_ref[...])
    ang = pos_ref[0].astype(F32) * invf_ref[...]
    cos = jnp.cos(ang)
    sin = jnp.sin(ang)
    lane = lax.broadcasted_iota(jnp.int32, ang.shape, 1)
    sin_signed = jnp.where(lane < ATTN_HEAD_DIM // 2, -sin, sin)

    def norm_rope(t, gain):
        y = _rms(t, gain)
        return y * cos + pltpu.roll(y, ATTN_HEAD_DIM // 2, axis=1) * sin_signed

    for h in range(ATTN_HEADS):
        lo = h * ATTN_HEAD_DIM
        q_ref[0, h] = norm_rope(qkv[:, lo:lo + ATTN_HEAD_DIM], qg_ref[...]).astype(BF16)
        k_ref[0, h] = norm_rope(qkv[:, D_MODEL + lo:D_MODEL + lo + ATTN_HEAD_DIM],
                                kg_ref[...]).astype(BF16)
        v_ref[0, h] = qkv[:, 2 * D_MODEL + lo:2 * D_MODEL + lo + ATTN_HEAD_DIM].astype(BF16)


def _qkv(x, g, w, q_gain, k_gain, pos3, inv_freq):
    b, t, _ = x.shape
    head_out = pl.BlockSpec((1, ATTN_HEADS, PROJ_ROWS, ATTN_HEAD_DIM), lambda i, j: (i, 0, j, 0))
    shape = jax.ShapeDtypeStruct((b, ATTN_HEADS, t, ATTN_HEAD_DIM), BF16)
    return pl.pallas_call(
        _qkv_kernel,
        grid=(b, t // PROJ_ROWS),
        in_specs=[pl.BlockSpec((1, PROJ_ROWS, D_MODEL), lambda i, j: (i, j, 0)),
                  _resident((1, D_MODEL)), _resident((D_MODEL, 3 * D_MODEL)),
                  _resident((1, ATTN_HEAD_DIM)), _resident((1, ATTN_HEAD_DIM)),
                  pl.BlockSpec((1, PROJ_ROWS, 1), lambda i, j: (i, j, 0)),
                  _resident((1, ATTN_HEAD_DIM))],
        out_specs=[head_out, head_out, head_out],
        out_shape=[shape, shape, shape],
        compiler_params=_params(2),
        name="qkv_rope",
    )(x, g, w, q_gain, k_gain, pos3, inv_freq)


def _moba_kernel(q_ref, k_ref, v_ref, o_ref, kaug_ref, kmean_ref):
    qi = pl.program_id(2)
    n_blk = k_ref.shape[2] // MOBA_BLOCK
    scale = ATTN_HEAD_DIM ** -0.5

    @pl.when(qi == 0)
    def _():
        k_all = k_ref[0, 0]
        row_blk = lax.broadcasted_iota(jnp.int32, (k_all.shape[0], LANES), 0) // MOBA_BLOCK
        lane = lax.broadcasted_iota(jnp.int32, (k_all.shape[0], LANES), 1)
        kaug_ref[:, :ATTN_HEAD_DIM] = k_all
        kaug_ref[:, ATTN_HEAD_DIM:] = jnp.where(lane == row_blk, 1.0, 0.0).astype(BF16)
        means = [jnp.mean(k_ref[0, 0, n * MOBA_BLOCK:(n + 1) * MOBA_BLOCK, :].astype(F32),
                          axis=0, keepdims=True) for n in range(n_blk)]
        kmean = jnp.concatenate(means, axis=0)
        kmean_ref[...] = jnp.concatenate([kmean] * (LANES // n_blk), axis=0).astype(BF16)

    q = q_ref[0, 0]
    gate = _dot_nt(q, kmean_ref[...])
    lane = lax.broadcasted_iota(jnp.int32, gate.shape, 1)
    lane_blk = lane & (n_blk - 1)
    past = lane_blk < qi
    gate = jnp.where(past, gate, -jnp.inf)
    rank = jnp.zeros(gate.shape, jnp.int32)
    for r in range(1, n_blk):
        other = pltpu.roll(gate, r, axis=1)
        other_blk = (lane_blk + (n_blk - r)) & (n_blk - 1)
        beats = (other > gate) | ((other == gate) & (other_blk < lane_blk))
        rank = rank + beats.astype(jnp.int32)
    chosen = (past & (rank < MOBA_TOPK)) | (lane_blk == qi)
    bias = jnp.where(chosen | (lane >= n_blk), 0.0, MASK_BIAS).astype(BF16)
    q_aug = jnp.concatenate([q, bias], axis=1)

    def scores(kb):
        start = pl.multiple_of(kb * MOBA_BLOCK, MOBA_BLOCK)
        k_blk = kaug_ref[pl.ds(start, MOBA_BLOCK), :]
        v_blk = v_ref[0, 0, pl.ds(start, MOBA_BLOCK), :]
        return _dot_nt(q_aug, k_blk) * scale, v_blk

    s, v_blk = scores(qi)
    row = lax.broadcasted_iota(jnp.int32, s.shape, 0)
    col = lax.broadcasted_iota(jnp.int32, s.shape, 1)
    s = jnp.where(col <= row, s, -jnp.inf)
    m = jnp.max(s, axis=-1, keepdims=True)
    p = jnp.exp(s - m)
    l = jnp.sum(p, axis=-1, keepdims=True)
    acc = _dot(p.astype(BF16), v_blk)

    def body(kb, carry):
        m, l, acc = carry
        s, v_blk = scores(kb)
        m_new = jnp.maximum(m, jnp.max(s, axis=-1, keepdims=True))
        alpha = jnp.exp(m - m_new)
        p = jnp.exp(s - m_new)
        l = alpha * l + jnp.sum(p, axis=-1, keepdims=True)
        acc = alpha * acc + _dot(p.astype(BF16), v_blk)
        return m_new, l, acc

    m, l, acc = lax.fori_loop(0, qi, body, (m, l, acc))
    o_ref[0] = (acc / l).astype(BF16)


def _moba(q, k, v):
    b, h, t, d = q.shape
    kv_spec = pl.BlockSpec((1, 1, t, d), lambda i, j, s: (i, j, 0, 0))
    return pl.pallas_call(
        _moba_kernel,
        grid=(b, h, t // MOBA_BLOCK),
        in_specs=[pl.BlockSpec((1, 1, MOBA_BLOCK, d), lambda i, j, s: (i, j, s, 0)),
                  kv_spec, kv_spec],
        out_specs=pl.BlockSpec((1, MOBA_BLOCK, d), lambda i, j, s: (i, s, j)),
        out_shape=jax.ShapeDtypeStruct((b, t, h * d), BF16),
        scratch_shapes=[pltpu.VMEM((t, 2 * d), BF16), pltpu.VMEM((LANES, d), BF16)],
        compiler_params=_params(3),
        name="moba_attn",
    )(q, k, v)


def _ssm_in_kernel(x_ref, halo_ref, g_ref, wz_ref, wx_ref, wdt_ref, cw_ref, cb_ref, dtb_ref,
                   z_ref, xs_ref, b_ref, c_ref, dt_ref, conv_ref):
    j = pl.program_id(1)
    rows = x_ref.shape[1]
    g = g_ref[...]
    n = _rms(x_ref[0], g).astype(BF16)
    n_halo = _rms(halo_ref[0], g).astype(BF16)
    z_ref[0] = _dot(n, wz_ref[...])

    halo_keep = jnp.where(j > 0, 1.0, 0.0)
    conv_ref[:HALO_ROWS, :] = _dot(n_halo, wx_ref[...]) * halo_keep
    conv_ref[HALO_ROWS:, :] = _dot(n, wx_ref[...])
    y = cb_ref[...]
    for tap in range(SSM_CONV):
        start = HALO_ROWS - (SSM_CONV - 1) + tap
        y = y + conv_ref[pl.ds(start, rows), :] * cw_ref[tap:tap + 1, :]
    y = _silu(y)
    xs_ref[0] = y[:, :SSM_D_INNER]
    b_ref[0] = y[:, SSM_D_INNER:SSM_D_INNER + SSM_BC_DIM].astype(BF16)
    c_ref[0] = y[:, SSM_D_INNER + SSM_BC_DIM:].astype(BF16)

    dt_raw = _dot(n, wdt_ref[...]) + dtb_ref[...]
    softplus = jnp.maximum(dt_raw, 0.0) + jnp.log1p(jnp.exp(-jnp.abs(dt_raw)))
    lane = lax.broadcasted_iota(jnp.int32, dt_raw.shape, 1)
    dt_ref[0] = jnp.where(lane < SSM_HEADS, softplus, 0.0)


def _ssm_in(x, g, wz, wx, wdt, conv_w, conv_b, dt_bias):
    b, t, _ = x.shape
    rows = SSM_IN_ROWS
    halo_blocks = rows // HALO_ROWS

    def tok(width, dtype):
        return (pl.BlockSpec((1, rows, width), lambda i, j: (i, j, 0)),
                jax.ShapeDtypeStruct((b, t, width), dtype))

    outs = [tok(SSM_D_INNER, F32), tok(SSM_D_INNER, F32), tok(SSM_BC_DIM, BF16),
            tok(SSM_BC_DIM, BF16), tok(LANES, F32)]
    return pl.pallas_call(
        _ssm_in_kernel,
        grid=(b, t // rows),
        in_specs=[pl.BlockSpec((1, rows, D_MODEL), lambda i, j: (i, j, 0)),
                  pl.BlockSpec((1, HALO_ROWS, D_MODEL),
                               lambda i, j: (i, jnp.maximum(j * halo_blocks - 1, 0), 0)),
                  _resident((1, D_MODEL)), _resident((D_MODEL, SSM_D_INNER)),
                  _resident((D_MODEL, SSM_CONV_DIM)), _resident((D_MODEL, LANES)),
                  _resident((SSM_CONV, SSM_CONV_DIM)), _resident((1, SSM_CONV_DIM)),
                  _resident((1, LANES))],
        out_specs=[o[0] for o in outs],
        out_shape=[o[1] for o in outs],
        scratch_shapes=[pltpu.VMEM((rows + HALO_ROWS, SSM_CONV_DIM), F32)],
        compiler_params=_params(2),
        name="ssm_in",
    )(x, x, g, wz, wx, wdt, conv_w, conv_b, dt_bias)


def _ssd_kernel(xs_ref, b_ref, c_ref, dt_ref, z_ref, alog_ref, dskip_ref, ng_ref,
                tri_ref, eye_ref, ehead_ref, ecol_ref, y_ref, state_ref, ybuf_ref):
    ch = SSM_CHUNK
    heads_per_group = SSM_HEADS // SSM_GROUPS
    group_w = heads_per_group * SSM_HEAD_DIM

    @pl.when(pl.program_id(1) == 0)
    def _():
        state_ref[...] = jnp.zeros(state_ref.shape, F32)

    dt = dt_ref[0]
    a = dt * (-jnp.exp(alog_ref[...]))
    a_cs = _dot(tri_ref[...], jnp.concatenate(_split3(a), axis=0))
    a_last = a_cs[ch - 1:ch, :]
    decay = jnp.exp(a_last - a_cs)
    tail = jnp.concatenate([jnp.exp(a_last), dskip_ref[...],
                            jnp.zeros((HALO_ROWS - 2, LANES), F32)], axis=0)
    stack = jnp.concatenate([dt, decay, tail], axis=0)
    wide = _dot(jnp.concatenate(_split3(stack), axis=1), ehead_ref[...])
    dt_x = wide[:ch]
    decay_x = wide[ch:2 * ch]
    chunk_decay_x = wide[2 * ch:2 * ch + 1]
    dskip_x = wide[2 * ch + 1:2 * ch + 2]

    acs3 = jnp.concatenate(_split3(a_cs), axis=1)
    acs_col = _dot(acs3, ecol_ref[...])
    acs_row = _dot_nt(eye_ref[...], acs3)

    x = xs_ref[0]
    xdt = x * dt_x
    xdt_b = xdt.astype(BF16)
    xdd_b = (xdt * decay_x).astype(BF16)
    tril = (lax.broadcasted_iota(jnp.int32, (ch, ch), 0)
            >= lax.broadcasted_iota(jnp.int32, (ch, ch), 1))
    lane = lax.broadcasted_iota(jnp.int32, (ch, LANES), 1)
    eye = eye_ref[:, :LANES]

    for g in range(SSM_GROUPS):
        bg = b_ref[0, :, g * SSM_STATE:(g + 1) * SSM_STATE]
        cg = c_ref[0, :, g * SSM_STATE:(g + 1) * SSM_STATE]
        cb = _dot_nt(cg, bg)
        bg_t = _dot_nt(eye, bg).astype(BF16)
        gl = g * group_w
        state_g = state_ref[:, gl:gl + group_w]
        lhs = []
        for r in range(heads_per_group):
            h = g * heads_per_group + r
            col = acs_col[:, h * LANES:(h + 1) * LANES]
            seg = col - acs_row[h:h + 1, :]
            decay_ls = jnp.exp(jnp.where(tril, seg, -jnp.inf))
            lhs.append(jnp.concatenate([(cb * decay_ls).astype(BF16),
                                        (cg.astype(F32) * jnp.exp(col)).astype(BF16)], axis=1))
        for p in range(heads_per_group // 2):
            pl_lo = gl + p * LANES
            rhs = jnp.concatenate([xdt_b[:, pl_lo:pl_lo + LANES],
                                   state_g[:, p * LANES:(p + 1) * LANES].astype(BF16)], axis=0)
            y0 = _dot(lhs[2 * p], rhs)
            y1 = _dot(lhs[2 * p + 1], rhs)
            ybuf_ref[:, pl_lo:pl_lo + LANES] = jnp.where(lane < SSM_HEAD_DIM, y0, y1)
        new_state = _dot(bg_t, xdd_b[:, gl:gl + group_w])
        state_ref[:, gl:gl + group_w] = state_g * chunk_decay_x[:, gl:gl + group_w] + new_state

    z = z_ref[0]
    y = (ybuf_ref[...] + dskip_x * x) * _silu(z)
    for g in range(SSM_GROUPS):
        gl = g * group_w
        yg = y[:, gl:gl + group_w]
        ms = jnp.mean(yg * yg, axis=-1, keepdims=True)
        y_ref[0, :, gl:gl + group_w] = (yg * lax.rsqrt(ms + EPS)
                                        * ng_ref[:, gl:gl + group_w]).astype(BF16)


def _ssd_constants():
    ch = SSM_CHUNK
    tri = np.tril(np.ones((ch, ch), np.float32))
    eye = np.eye(LANES, dtype=np.float32)
    ehead = np.zeros((LANES, SSM_D_INNER), np.float32)
    ecol = np.zeros((LANES, SSM_HEADS * LANES), np.float32)
    for h in range(SSM_HEADS):
        ehead[h, h * SSM_HEAD_DIM:(h + 1) * SSM_HEAD_DIM] = 1.0
        ecol[h, h * LANES:(h + 1) * LANES] = 1.0
    as_bf16 = lambda m: jnp.asarray(m, BF16)
    return (as_bf16(np.tile(tri, (1, 3))), as_bf16(np.tile(eye, (1, 3))),
            as_bf16(np.tile(ehead, (3, 1))), as_bf16(np.tile(ecol, (3, 1))))


def _ssd(xs, bm, cm, dt, z, a_log, d_skip, norm_g):
    b, t, _ = xs.shape
    ch = SSM_CHUNK
    tri3, eye3, ehead3, ecol3 = _ssd_constants()

    def tok(width):
        return pl.BlockSpec((1, ch, width), lambda i, j: (i, j, 0))

    return pl.pallas_call(
        _ssd_kernel,
        grid=(b, t // ch),
        in_specs=[tok(SSM_D_INNER), tok(SSM_BC_DIM), tok(SSM_BC_DIM), tok(LANES), tok(SSM_D_INNER),
                  _resident((1, LANES)), _resident((1, LANES)), _resident((1, SSM_D_INNER)),
                  _resident(tri3.shape), _resident(eye3.shape), _resident(ehead3.shape),
                  _resident(ecol3.shape)],
        out_specs=tok(SSM_D_INNER),
        out_shape=jax.ShapeDtypeStruct((b, t, SSM_D_INNER), BF16),
        scratch_shapes=[pltpu.VMEM((SSM_STATE, SSM_D_INNER), F32),
                        pltpu.VMEM((ch, SSM_D_INNER), F32)],
        compiler_params=_params(2),
        name="ssd",
    )(xs, bm, cm, dt, z, a_log, d_skip, norm_g, tri3, eye3, ehead3, ecol3)


def _pad_lanes(v):
    return jnp.pad(v.astype(F32), (0, LANES - v.shape[0]))[None, :]


def kernel(x, positions, ffn1_norm, ffn1_w_gate, ffn1_w_up, ffn1_w_down, mix_norm, ffn2_norm, ffn2_w_gate, ffn2_w_up, ffn2_w_down, attn_w_qkv, attn_q_norm, attn_k_norm, attn_w_o, ssm_w_in, ssm_conv_w, ssm_conv_b, ssm_dt_bias, ssm_a_log, ssm_d, ssm_norm, ssm_w_out):
    bsz, seq, d = x.shape
    rows = bsz * seq
    half = ATTN_HEAD_DIM // 2
    inv_freq = ROPE_THETA ** (-jnp.arange(half, dtype=F32) / half)
    inv_freq = jnp.concatenate([inv_freq, inv_freq])[None, :]
    pos3 = positions[:, :, None]

    def ffn(xc, norm, wg, wu, wd):
        out = _ffn(xc.reshape(rows, d), norm[None, :], wg.astype(BF16), wu.astype(BF16),
                   wd.astype(BF16))
        return out.reshape(bsz, seq, d)

    x = ffn(x, ffn1_norm[0], ffn1_w_gate[0], ffn1_w_up[0], ffn1_w_down[0])
    q, k, v = _qkv(x, mix_norm[0][None, :], attn_w_qkv[0].astype(BF16), attn_q_norm[0][None, :],
                   attn_k_norm[0][None, :], pos3, inv_freq)
    o = _moba(q, k, v)
    x = _proj_res(x.reshape(rows, d), o.reshape(rows, d), attn_w_o[0].astype(BF16))
    x = ffn(x.reshape(bsz, seq, d), ffn2_norm[0], ffn2_w_gate[0], ffn2_w_up[0], ffn2_w_down[0])

    x = ffn(x, ffn1_norm[1], ffn1_w_gate[1], ffn1_w_up[1], ffn1_w_down[1])
    w_in = ssm_w_in[0]
    wz = w_in[:, :SSM_D_INNER].astype(BF16)
    wx = w_in[:, SSM_D_INNER:SSM_D_INNER + SSM_CONV_DIM].astype(BF16)
    wdt = jnp.pad(w_in[:, SSM_D_INNER + SSM_CONV_DIM:], ((0, 0), (0, LANES - SSM_HEADS))).astype(BF16)
    z, xs, bm, cm, dt = _ssm_in(x, mix_norm[1][None, :], wz, wx, wdt, ssm_conv_w[0],
                                ssm_conv_b[0][None, :], _pad_lanes(ssm_dt_bias[0]))
    y = _ssd(xs, bm, cm, dt, z, _pad_lanes(ssm_a_log[0]), _pad_lanes(ssm_d[0]),
             ssm_norm[0][None, :])
    x = _proj_res(x.reshape(rows, d), y.reshape(rows, SSM_D_INNER), ssm_w_out[0].astype(BF16))
    x = ffn(x.reshape(bsz, seq, d), ffn2_norm[1], ffn2_w_gate[1], ffn2_w_up[1], ffn2_w_down[1])
    return x
```

```python
import functools
import math

import numpy as np
import jax
import jax.numpy as jnp
from jax import lax
from jax.experimental import pallas as pl
from jax.experimental.pallas import tpu as pltpu

F32 = jnp.float32
BF16 = jnp.bfloat16

D_MODEL = 1024
D_FF = 2816
EPS = 1e-6

ATTN_HEADS = 8
ATTN_HEAD_DIM = 128
MOBA_BLOCK = 256
MOBA_TOPK = 3
ROPE_THETA = 10000.0

SSM_D_INNER = 2048
SSM_HEAD_DIM = 64
SSM_HEADS = 32
SSM_GROUPS = 8
SSM_STATE = 128
SSM_CONV = 4
SSM_CHUNK = 128
SSM_BC_DIM = SSM_GROUPS * SSM_STATE
SSM_CONV_DIM = SSM_D_INNER + 2 * SSM_BC_DIM

LANES = 128
VMEM_LIMIT_BYTES = 56 * 1024 * 1024

FFN_ROWS = 512
FFN_COLS = 256
PROJ_ROWS = 512
SSM_IN_ROWS = 256
HALO_ROWS = 8


def _resident(shape):
    nd = len(shape)
    return pl.BlockSpec(shape, lambda *_: (0,) * nd, pipeline_mode=pl.Buffered(1))


def _params(n_axes):
    return pltpu.CompilerParams(dimension_semantics=("arbitrary",) * n_axes,
                                vmem_limit_bytes=VMEM_LIMIT_BYTES)


def _rms(x, g):
    ms = jnp.mean(x * x, axis=-1, keepdims=True)
    return x * lax.rsqrt(ms + EPS) * g


def _silu(x):
    return x * jax.nn.sigmoid(x)


def _dot(a, b):
    return jnp.dot(a, b, preferred_element_type=F32)


def _dot_nt(a, b):
    return lax.dot_general(a, b, (((1,), (1,)), ((), ())), preferred_element_type=F32)


def _split3(v):
    hi = v.astype(BF16)
    r1 = v - hi.astype(F32)
    mid = r1.astype(BF16)
    lo = (r1 - mid.astype(F32)).astype(BF16)
    return hi, mid, lo


def _ffn_kernel(x_ref, g_ref, wg_ref, wu_ref, wd_ref, o_ref):
    x = x_ref[...]
    n = _rms(x, g_ref[...]).astype(BF16)
    acc = jnp.zeros(x.shape, F32)
    for c in range(D_FF // FFN_COLS):
        sl = slice(c * FFN_COLS, (c + 1) * FFN_COLS)
        gate = _dot(n, wg_ref[:, sl])
        up = _dot(n, wu_ref[:, sl])
        h = (_silu(gate) * up).astype(BF16)
        acc = acc + _dot(h, wd_ref[sl, :])
    o_ref[...] = x + 0.5 * acc


def _ffn(x2d, g, wg, wu, wd):
    m = x2d.shape[0]
    row = pl.BlockSpec((FFN_ROWS, D_MODEL), lambda i: (i, 0))
    return pl.pallas_call(
        _ffn_kernel,
        grid=(m // FFN_ROWS,),
        in_specs=[row, _resident((1, D_MODEL)), _resident((D_MODEL, D_FF)),
                  _resident((D_MODEL, D_FF)), _resident((D_FF, D_MODEL))],
        out_specs=row,
        out_shape=jax.ShapeDtypeStruct(x2d.shape, F32),
        compiler_params=_params(1),
        name="ffn",
    )(x2d, g, wg, wu, wd)


def _proj_res_kernel(x_ref, a_ref, w_ref, o_ref):
    o_ref[...] = x_ref[...] + _dot(a_ref[...], w_ref[...])


def _proj_res(x2d, a2d, w):
    m, k = a2d.shape
    return pl.pallas_call(
        _proj_res_kernel,
        grid=(m // PROJ_ROWS,),
        in_specs=[pl.BlockSpec((PROJ_ROWS, D_MODEL), lambda i: (i, 0)),
                  pl.BlockSpec((PROJ_ROWS, k), lambda i: (i, 0)),
                  _resident((k, D_MODEL))],
        out_specs=pl.BlockSpec((PROJ_ROWS, D_MODEL), lambda i: (i, 0)),
        out_shape=jax.ShapeDtypeStruct(x2d.shape, F32),
        compiler_params=_params(1),
        name="proj_res",
    )(x2d, a2d, w)


def _qkv_kernel(x_ref, g_ref, w_ref, qg_ref, kg_ref, pos_ref, invf_ref, q_ref, k_ref, v_ref):
    n = _rms(x_ref[0], g_ref[...]).astype(BF16)
    qkv = _dot(n, w_ref[...])
    ang = pos_ref[0].astype(F32) * invf_ref[...]
    cos = jnp.cos(ang)
    sin = jnp.sin(ang)
    lane = lax.broadcasted_iota(jnp.int32, ang.shape, 1)
    sin_signed = jnp.where(lane < ATTN_HEAD_DIM // 2, -sin, sin)

    def norm_rope(t, gain):
        y = _rms(t, gain)
        return y * cos + pltpu.roll(y, ATTN_HEAD_DIM // 2, axis=1) * sin_signed

    for h in range(ATTN_HEADS):
        lo = h * ATTN_HEAD_DIM
        q_ref[0, h] = norm_rope(qkv[:, lo:lo + ATTN_HEAD_DIM], qg_ref[...]).astype(BF16)
        k_ref[0, h] = norm_rope(qkv[:, D_MODEL + lo:D_MODEL + lo + ATTN_HEAD_DIM],
                                kg_ref[...]).astype(BF16)
        v_ref[0, h] = qkv[:, 2 * D_MODEL + lo:2 * D_MODEL + lo + ATTN_HEAD_DIM].astype(BF16)


def _qkv(x, g, w, q_gain, k_gain, pos3, inv_freq):
    b, t, _ = x.shape
    head_out = pl.BlockSpec((1, ATTN_HEADS, PROJ_ROWS, ATTN_HEAD_DIM), lambda i, j: (i, 0, j, 0))
    shape = jax.ShapeDtypeStruct((b, ATTN_HEADS, t, ATTN_HEAD_DIM), BF16)
    return pl.pallas_call(
        _qkv_kernel,
        grid=(b, t // PROJ_ROWS),
        in_specs=[pl.BlockSpec((1, PROJ_ROWS, D_MODEL), lambda i, j: (i, j, 0)),
                  _resident((1, D_MODEL)), _resident((D_MODEL, 3 * D_MODEL)),
                  _resident((1, ATTN_HEAD_DIM)), _resident((1, ATTN_HEAD_DIM)),
                  pl.BlockSpec((1, PROJ_ROWS, 1), lambda i, j: (i, j, 0)),
                  _resident((1, ATTN_HEAD_DIM))],
        out_specs=[head_out, head_out, head_out],
        out_shape=[shape, shape, shape],
        compiler_params=_params(2),
        name="qkv_rope",
    )(x, g, w, q_gain, k_gain, pos3, inv_freq)


def _eye(n):
    return jnp.where(lax.broadcasted_iota(jnp.int32, (n, n), 0)
                     == lax.broadcasted_iota(jnp.int32, (n, n), 1), 1.0, 0.0).astype(BF16)


def _moba_kernel(q_ref, k_ref, v_ref, o_ref, vt_ref, s_ref, p_ref):
    blk = MOBA_BLOCK
    t = q_ref.shape[2]
    n_blk = t // blk
    scale = ATTN_HEAD_DIM ** -0.5
    q_all = q_ref[0, 0]
    vt_ref[...] = _dot_nt(_eye(ATTN_HEAD_DIM), v_ref[0, 0]).astype(BF16)

    means = [jnp.mean(k_ref[0, 0, n * blk:(n + 1) * blk, :].astype(F32), axis=0, keepdims=True)
             for n in range(n_blk)]
    gate = _dot_nt(jnp.concatenate(means, axis=0).astype(BF16), q_all)
    key_blk = lax.broadcasted_iota(jnp.int32, gate.shape, 0)
    qry_blk = lax.broadcasted_iota(jnp.int32, gate.shape, 1) // blk
    past = key_blk < qry_blk
    gate = jnp.where(past, gate, -jnp.inf)
    rank = jnp.zeros(gate.shape, F32)
    for n in range(n_blk):
        other = gate[n:n + 1, :]
        rank = (rank + jnp.where(other > gate, 1.0, 0.0)
                + jnp.where(other == gate, jnp.where(key_blk > n, 1.0, 0.0), 0.0))
    bias = jnp.where(past & (rank < MOBA_TOPK), 0.0, -jnp.inf)

    c = scale * math.log2(math.e)
    causal = (lax.broadcasted_iota(jnp.int32, (blk, blk), 0)
              <= lax.broadcasted_iota(jnp.int32, (blk, blk), 1))
    eye_q = _eye(blk)

    def raw_scores(qi):
        n_keys = (qi + 1) * blk
        s_ref[qi % 2, :n_keys, :] = _dot_nt(k_ref[0, 0, :n_keys, :],
                                            q_ref[0, 0, qi * blk:n_keys, :])

    raw_scores(0)
    for qi in range(n_blk):
        if qi + 1 < n_blk:
            raw_scores(qi + 1)
        slot = qi % 2
        q_lo = qi * blk
        n_keys = q_lo + blk
        rows = lambda n: slice(n * blk, (n + 1) * blk)
        diag = jnp.where(causal, s_ref[slot, rows(qi), :], -jnp.inf)
        maxes = [jnp.max(s_ref[slot, rows(n), :], axis=0, keepdims=True)
                 + bias[n:n + 1, q_lo:n_keys] for n in range(qi)]
        maxes.append(jnp.max(diag, axis=0, keepdims=True))
        m = functools.reduce(jnp.maximum, maxes)
        l = jnp.zeros((1, blk), F32)
        for n in range(qi + 1):
            if n < qi:
                p = jnp.exp2(s_ref[slot, rows(n), :] * c + (bias[n:n + 1, q_lo:n_keys] - m) * c)
            else:
                p = jnp.exp2(diag * c - m * c)
            l = l + jnp.sum(p, axis=0, keepdims=True)
            p_ref[slot, rows(n), :] = p.astype(BF16)
        o_t = _dot(vt_ref[:, :n_keys], p_ref[slot, :n_keys, :]) / l
        o_ref[0, q_lo:n_keys, :] = _dot_nt(eye_q, o_t.astype(BF16)).astype(BF16)


def _moba(q, k, v):
    b, h, t, d = q.shape
    head = pl.BlockSpec((1, 1, t, d), lambda i, j: (i, j, 0, 0))
    return pl.pallas_call(
        _moba_kernel,
        grid=(b, h),
        in_specs=[head, head, head],
        out_specs=pl.BlockSpec((1, t, d), lambda i, j: (i, 0, j)),
        out_shape=jax.ShapeDtypeStruct((b, t, h * d), BF16),
        scratch_shapes=[pltpu.VMEM((d, t), BF16), pltpu.VMEM((2, t, MOBA_BLOCK), F32),
                        pltpu.VMEM((2, t, MOBA_BLOCK), BF16)],
        compiler_params=_params(2),
        name="moba_attn",
    )(q, k, v)


def _ssm_in_kernel(x_ref, halo_ref, g_ref, wz_ref, wx_ref, wdt_ref, cw_ref, cb_ref, dtb_ref,
                   z_ref, xs_ref, b_ref, c_ref, dt_ref):
    j = pl.program_id(1)
    g = g_ref[...]
    n = _rms(x_ref[0], g).astype(BF16)
    n_halo = _rms(halo_ref[0], g).astype(BF16)
    z_ref[0] = _dot(n, wz_ref[...])

    halo_keep = jnp.where(j > 0, 1.0, 0.0)
    halo = _dot(n_halo, wx_ref[...]) * halo_keep
    cur = _dot(n, wx_ref[...])
    groups = cur.shape[0] // HALO_ROWS
    cur3 = cur.reshape(groups, HALO_ROWS, SSM_CONV_DIM)
    sub = lax.broadcasted_iota(jnp.int32, cur3.shape, 1)
    y = cb_ref[...] + cur3 * cw_ref[SSM_CONV - 1:SSM_CONV, :]
    for back in range(1, SSM_CONV):
        rot = pltpu.roll(cur3, back, axis=1)
        rot_prev = jnp.concatenate([pltpu.roll(halo, back, axis=0)[None], rot[:-1]], axis=0)
        shifted = jnp.where(sub >= back, rot, rot_prev)
        y = y + shifted * cw_ref[SSM_CONV - 1 - back:SSM_CONV - back, :]
    y = _silu(y).reshape(cur.shape)
    xs_ref[0] = y[:, :SSM_D_INNER]
    b_ref[0] = y[:, SSM_D_INNER:SSM_D_INNER + SSM_BC_DIM].astype(BF16)
    c_ref[0] = y[:, SSM_D_INNER + SSM_BC_DIM:].astype(BF16)

    dt_raw = _dot(n, wdt_ref[...]) + dtb_ref[...]
    softplus = jnp.maximum(dt_raw, 0.0) + jnp.log1p(jnp.exp(-jnp.abs(dt_raw)))
    lane = lax.broadcasted_iota(jnp.int32, dt_raw.shape, 1)
    dt_ref[0] = jnp.where(lane < SSM_HEADS, softplus, 0.0)


def _ssm_in(x, g, wz, wx, wdt, conv_w, conv_b, dt_bias):
    b, t, _ = x.shape
    rows = SSM_IN_ROWS
    halo_blocks = rows // HALO_ROWS

    def tok(width, dtype):
        return (pl.BlockSpec((1, rows, width), lambda i, j: (i, j, 0)),
                jax.ShapeDtypeStruct((b, t, width), dtype))

    outs = [tok(SSM_D_INNER, F32), tok(SSM_D_INNER, F32), tok(SSM_BC_DIM, BF16),
            tok(SSM_BC_DIM, BF16), tok(LANES, F32)]
    return pl.pallas_call(
        _ssm_in_kernel,
        grid=(b, t // rows),
        in_specs=[pl.BlockSpec((1, rows, D_MODEL), lambda i, j: (i, j, 0)),
                  pl.BlockSpec((1, HALO_ROWS, D_MODEL),
                               lambda i, j: (i, jnp.maximum(j * halo_blocks - 1, 0), 0)),
                  _resident((1, D_MODEL)), _resident((D_MODEL, SSM_D_INNER)),
                  _resident((D_MODEL, SSM_CONV_DIM)), _resident((D_MODEL, LANES)),
                  _resident((SSM_CONV, SSM_CONV_DIM)), _resident((1, SSM_CONV_DIM)),
                  _resident((1, LANES))],
        out_specs=[o[0] for o in outs],
        out_shape=[o[1] for o in outs],
        compiler_params=_params(2),
        name="ssm_in",
    )(x, x, g, wz, wx, wdt, conv_w, conv_b, dt_bias)


def _ssd_kernel(xs_ref, b_ref, c_ref, dt_ref, z_ref, alog_ref, dskip_ref, ng_ref,
                tri_ref, eye_ref, ehead_ref, ecol_ref, y_ref, state_ref, ybuf_ref):
    ch = SSM_CHUNK
    heads_per_group = SSM_HEADS // SSM_GROUPS
    group_w = heads_per_group * SSM_HEAD_DIM

    @pl.when(pl.program_id(1) == 0)
    def _():
        state_ref[...] = jnp.zeros(state_ref.shape, F32)

    dt = dt_ref[0]
    a = dt * (-jnp.exp(alog_ref[...]))
    a_cs = _dot(tri_ref[...], jnp.concatenate(_split3(a), axis=0))
    a_last = a_cs[ch - 1:ch, :]
    decay = jnp.exp(a_last - a_cs)
    tail = jnp.concatenate([jnp.exp(a_last), dskip_ref[...],
                            jnp.zeros((HALO_ROWS - 2, LANES), F32)], axis=0)
    stack = jnp.concatenate([dt, decay, tail], axis=0)
    wide = _dot(jnp.concatenate(_split3(stack), axis=1), ehead_ref[...])
    dt_x = wide[:ch]
    decay_x = wide[ch:2 * ch]
    chunk_decay_x = wide[2 * ch:2 * ch + 1]
    dskip_x = wide[2 * ch + 1:2 * ch + 2]

    acs3 = jnp.concatenate(_split3(a_cs), axis=1)
    acs_col = _dot(acs3, ecol_ref[...])
    acs_row = _dot_nt(eye_ref[...], acs3)

    x = xs_ref[0]
    xdt = x * dt_x
    xdt_b = xdt.astype(BF16)
    xdd_b = (xdt * decay_x).astype(BF16)
    tril = (lax.broadcasted_iota(jnp.int32, (ch, ch), 0)
            >= lax.broadcasted_iota(jnp.int32, (ch, ch), 1))
    lane = lax.broadcasted_iota(jnp.int32, (ch, LANES), 1)
    eye = eye_ref[:, :LANES]

    for g in range(SSM_GROUPS):
        bg = b_ref[0, :, g * SSM_STATE:(g + 1) * SSM_STATE]
        cg = c_ref[0, :, g * SSM_STATE:(g + 1) * SSM_STATE]
        cb = _dot_nt(cg, bg)
        bg_t = _dot_nt(eye, bg).astype(BF16)
        gl = g * group_w
        state_g = state_ref[:, gl:gl + group_w]
        lhs = []
        for r in range(heads_per_group):
            h = g * heads_per_group + r
            col = acs_col[:, h * LANES:(h + 1) * LANES]
            seg = col - acs_row[h:h + 1, :]
            decay_ls = jnp.exp(jnp.where(tril, seg, -jnp.inf))
            lhs.append(jnp.concatenate([(cb * decay_ls).astype(BF16),
                                        (cg.astype(F32) * jnp.exp(col)).astype(BF16)], axis=1))
        for p in range(heads_per_group // 2):
            pl_lo = gl + p * LANES
            rhs = jnp.concatenate([xdt_b[:, pl_lo:pl_lo + LANES],
                                   state_g[:, p * LANES:(p + 1) * LANES].astype(BF16)], axis=0)
            y0 = _dot(lhs[2 * p], rhs)
            y1 = _dot(lhs[2 * p + 1], rhs)
            ybuf_ref[:, pl_lo:pl_lo + LANES] = jnp.where(lane < SSM_HEAD_DIM, y0, y1)
        new_state = _dot(bg_t, xdd_b[:, gl:gl + group_w])
        state_ref[:, gl:gl + group_w] = state_g * chunk_decay_x[:, gl:gl + group_w] + new_state

    z = z_ref[0]
    y = (ybuf_ref[...] + dskip_x * x) * _silu(z)
    for g in range(SSM_GROUPS):
        gl = g * group_w
        yg = y[:, gl:gl + group_w]
        ms = jnp.mean(yg * yg, axis=-1, keepdims=True)
        y_ref[0, :, gl:gl + group_w] = (yg * lax.rsqrt(ms + EPS)
                                        * ng_ref[:, gl:gl + group_w]).astype(BF16)


def _ssd_constants():
    ch = SSM_CHUNK
    tri = np.tril(np.ones((ch, ch), np.float32))
    eye = np.eye(LANES, dtype=np.float32)
    ehead = np.zeros((LANES, SSM_D_INNER), np.float32)
    ecol = np.zeros((LANES, SSM_HEADS * LANES), np.float32)
    for h in range(SSM_HEADS):
        ehead[h, h * SSM_HEAD_DIM:(h + 1) * SSM_HEAD_DIM] = 1.0
        ecol[h, h * LANES:(h + 1) * LANES] = 1.0
    as_bf16 = lambda m: jnp.asarray(m, BF16)
    return (as_bf16(np.tile(tri, (1, 3))), as_bf16(np.tile(eye, (1, 3))),
            as_bf16(np.tile(ehead, (3, 1))), as_bf16(np.tile(ecol, (3, 1))))


def _ssd(xs, bm, cm, dt, z, a_log, d_skip, norm_g):
    b, t, _ = xs.shape
    ch = SSM_CHUNK
    tri3, eye3, ehead3, ecol3 = _ssd_constants()

    def tok(width):
        return pl.BlockSpec((1, ch, width), lambda i, j: (i, j, 0))

    return pl.pallas_call(
        _ssd_kernel,
        grid=(b, t // ch),
        in_specs=[tok(SSM_D_INNER), tok(SSM_BC_DIM), tok(SSM_BC_DIM), tok(LANES), tok(SSM_D_INNER),
                  _resident((1, LANES)), _resident((1, LANES)), _resident((1, SSM_D_INNER)),
                  _resident(tri3.shape), _resident(eye3.shape), _resident(ehead3.shape),
                  _resident(ecol3.shape)],
        out_specs=tok(SSM_D_INNER),
        out_shape=jax.ShapeDtypeStruct((b, t, SSM_D_INNER), BF16),
        scratch_shapes=[pltpu.VMEM((SSM_STATE, SSM_D_INNER), F32),
                        pltpu.VMEM((ch, SSM_D_INNER), F32)],
        compiler_params=_params(2),
        name="ssd",
    )(xs, bm, cm, dt, z, a_log, d_skip, norm_g, tri3, eye3, ehead3, ecol3)


def _pad_lanes(v):
    return jnp.pad(v.astype(F32), (0, LANES - v.shape[0]))[None, :]


def kernel(x, positions, ffn1_norm, ffn1_w_gate, ffn1_w_up, ffn1_w_down, mix_norm, ffn2_norm, ffn2_w_gate, ffn2_w_up, ffn2_w_down, attn_w_qkv, attn_q_norm, attn_k_norm, attn_w_o, ssm_w_in, ssm_conv_w, ssm_conv_b, ssm_dt_bias, ssm_a_log, ssm_d, ssm_norm, ssm_w_out):
    bsz, seq, d = x.shape
    rows = bsz * seq
    half = ATTN_HEAD_DIM // 2
    inv_freq = ROPE_THETA ** (-jnp.arange(half, dtype=F32) / half)
    inv_freq = jnp.concatenate([inv_freq, inv_freq])[None, :]
    pos3 = positions[:, :, None]

    def ffn(xc, norm, wg, wu, wd):
        out = _ffn(xc.reshape(rows, d), norm[None, :], wg.astype(BF16), wu.astype(BF16),
                   wd.astype(BF16))
        return out.reshape(bsz, seq, d)

    x = ffn(x, ffn1_norm[0], ffn1_w_gate[0], ffn1_w_up[0], ffn1_w_down[0])
    q, k, v = _qkv(x, mix_norm[0][None, :], attn_w_qkv[0].astype(BF16), attn_q_norm[0][None, :],
                   attn_k_norm[0][None, :], pos3, inv_freq)
    o = _moba(q, k, v)
    x = _proj_res(x.reshape(rows, d), o.reshape(rows, d), attn_w_o[0].astype(BF16))
    x = ffn(x.reshape(bsz, seq, d), ffn2_norm[0], ffn2_w_gate[0], ffn2_w_up[0], ffn2_w_down[0])

    x = ffn(x, ffn1_norm[1], ffn1_w_gate[1], ffn1_w_up[1], ffn1_w_down[1])
    w_in = ssm_w_in[0]
    wz = w_in[:, :SSM_D_INNER].astype(BF16)
    wx = w_in[:, SSM_D_INNER:SSM_D_INNER + SSM_CONV_DIM].astype(BF16)
    wdt = jnp.pad(w_in[:, SSM_D_INNER + SSM_CONV_DIM:], ((0, 0), (0, LANES - SSM_HEADS))).astype(BF16)
    z, xs, bm, cm, dt = _ssm_in(x, mix_norm[1][None, :], wz, wx, wdt, ssm_conv_w[0],
                                ssm_conv_b[0][None, :], _pad_lanes(ssm_dt_bias[0]))
    y = _ssd(xs, bm, cm, dt, z, _pad_lanes(ssm_a_log[0]), _pad_lanes(ssm_d[0]),
             ssm_norm[0][None, :])
    x = _proj_res(x.reshape(rows, d), y.reshape(rows, SSM_D_INNER), ssm_w_out[0].astype(BF16))
    x = ffn(x.reshape(bsz, seq, d), ffn2_norm[1], ffn2_w_gate[1], ffn2_w_up[1], ffn2_w_down[1])
    return x
```

```python
import functools
import math

import numpy as np
import jax
import jax.numpy as jnp
from jax import lax
from jax.experimental import pallas as pl
from jax.experimental.pallas import tpu as pltpu

F32 = jnp.float32
BF16 = jnp.bfloat16

D_MODEL = 1024
D_FF = 2816
EPS = 1e-6

ATTN_HEADS = 8
ATTN_HEAD_DIM = 128
MOBA_BLOCK = 256
MOBA_TOPK = 3
ROPE_THETA = 10000.0

SSM_D_INNER = 2048
SSM_HEAD_DIM = 64
SSM_HEADS = 32
SSM_GROUPS = 8
SSM_STATE = 128
SSM_CONV = 4
SSM_CHUNK = 128
SSM_BC_DIM = SSM_GROUPS * SSM_STATE
SSM_CONV_DIM = SSM_D_INNER + 2 * SSM_BC_DIM

LANES = 128
VMEM_LIMIT_BYTES = 56 * 1024 * 1024

FFN_ROWS = 512
FFN_COLS = 256
PROJ_ROWS = 512
SSM_IN_ROWS = 256
SSM_IN_COLS = 256
SUBLANES = 8
HALO_ROWS = 16


def _resident(shape):
    nd = len(shape)
    return pl.BlockSpec(shape, lambda *_: (0,) * nd, pipeline_mode=pl.Buffered(1))


def _params(n_axes):
    return pltpu.CompilerParams(dimension_semantics=("arbitrary",) * n_axes,
                                vmem_limit_bytes=VMEM_LIMIT_BYTES)


def _rms(x, g):
    ms = jnp.mean(x * x, axis=-1, keepdims=True)
    return x * lax.rsqrt(ms + EPS) * g


def _silu(x):
    return x * jax.nn.sigmoid(x)


def _dot(a, b):
    return jnp.dot(a, b, preferred_element_type=F32)


def _dot_nt(a, b):
    return lax.dot_general(a, b, (((1,), (1,)), ((), ())), preferred_element_type=F32)


def _split2(v):
    hi = v.astype(BF16)
    return hi, (v - hi.astype(F32)).astype(BF16)


def _split3(v):
    hi = v.astype(BF16)
    r1 = v - hi.astype(F32)
    mid = r1.astype(BF16)
    lo = (r1 - mid.astype(F32)).astype(BF16)
    return hi, mid, lo


def _ffn_kernel(*refs, has_mixer_out):
    if has_mixer_out:
        x_ref, a_ref, wp_ref, g_ref, wg_ref, wu_ref, wd_ref, o_ref = refs
        x = x_ref[...] + _dot(a_ref[...], wp_ref[...])
    else:
        x_ref, g_ref, wg_ref, wu_ref, wd_ref, o_ref = refs
        x = x_ref[...]
    n = _rms(x, g_ref[...]).astype(BF16)
    acc = jnp.zeros(x.shape, F32)
    for c in range(D_FF // FFN_COLS):
        sl = slice(c * FFN_COLS, (c + 1) * FFN_COLS)
        gate = _dot(n, wg_ref[:, sl])
        up = _dot(n, wu_ref[:, sl])
        h = (_silu(gate) * up).astype(BF16)
        acc = acc + _dot(h, wd_ref[sl, :])
    o_ref[...] = x + 0.5 * acc


def _ffn(x2d, g, wg, wu, wd, mixer_out=None):
    m = x2d.shape[0]
    row = pl.BlockSpec((FFN_ROWS, D_MODEL), lambda i: (i, 0))
    weights = [_resident((1, D_MODEL)), _resident((D_MODEL, D_FF)),
               _resident((D_MODEL, D_FF)), _resident((D_FF, D_MODEL))]
    if mixer_out is None:
        operands, specs = (x2d, g, wg, wu, wd), [row] + weights
    else:
        a2d, wp = mixer_out
        k = a2d.shape[1]
        operands = (x2d, a2d, wp, g, wg, wu, wd)
        specs = [row, pl.BlockSpec((FFN_ROWS, k), lambda i: (i, 0)),
                 _resident((k, D_MODEL))] + weights
    return pl.pallas_call(
        functools.partial(_ffn_kernel, has_mixer_out=mixer_out is not None),
        grid=(m // FFN_ROWS,),
        in_specs=specs,
        out_specs=row,
        out_shape=jax.ShapeDtypeStruct(x2d.shape, F32),
        compiler_params=_params(1),
        name="ffn",
    )(*operands)


def _qkv_kernel(x_ref, g_ref, w_ref, qg_ref, kg_ref, pos_ref, invf_ref, hmean_ref,
                q_ref, k_ref, v_ref):
    n = _rms(x_ref[0], g_ref[...]).astype(BF16)
    qkv = _dot(n, w_ref[...])
    half_dim = ATTN_HEAD_DIM // 2
    half_rows = pos_ref.shape[1] // 2
    pos = pos_ref[0].astype(F32)
    lane = lax.broadcasted_iota(jnp.int32, (half_rows, ATTN_HEAD_DIM), 1)
    low = lane < half_dim
    ang = jnp.where(low, pos[:half_rows], pos[half_rows:]) * invf_ref[...]

    def spread(t):
        swapped = pltpu.roll(t, half_dim, axis=1)
        return jnp.concatenate([jnp.where(low, t, swapped), jnp.where(low, swapped, t)], axis=0)

    cos = spread(jnp.cos(ang))
    sin = spread(jnp.sin(ang))
    lane = lax.broadcasted_iota(jnp.int32, cos.shape, 1)
    sin_signed = jnp.where(lane < half_dim, -sin, sin)

    def norm_rope_pair(t2, gain, out_ref, h0):
        ms = _dot(jnp.concatenate(_split2(t2 * t2), axis=1), hmean_ref[...])
        y2 = t2 * lax.rsqrt(ms + EPS)
        for i in range(2):
            y = y2[:, i * ATTN_HEAD_DIM:(i + 1) * ATTN_HEAD_DIM] * gain
            out_ref[0, h0 + i] = (y * cos + pltpu.roll(y, half_dim, axis=1) * sin_signed).astype(BF16)

    pair = 2 * ATTN_HEAD_DIM
    for h0 in range(0, ATTN_HEADS, 2):
        lo = h0 * ATTN_HEAD_DIM
        norm_rope_pair(qkv[:, lo:lo + pair], qg_ref[...], q_ref, h0)
        norm_rope_pair(qkv[:, D_MODEL + lo:D_MODEL + lo + pair], kg_ref[...], k_ref, h0)
        for h in (h0, h0 + 1):
            v_lo = 2 * D_MODEL + h * ATTN_HEAD_DIM
            v_ref[0, h] = qkv[:, v_lo:v_lo + ATTN_HEAD_DIM].astype(BF16)


def _qkv(x, g, w, q_gain, k_gain, pos3, inv_freq):
    b, t, _ = x.shape
    pair = 2 * ATTN_HEAD_DIM
    same_head = (np.arange(2 * pair)[:, None] % pair) // ATTN_HEAD_DIM == np.arange(pair)[None, :] // ATTN_HEAD_DIM
    head_mean = jnp.asarray(same_head.astype(np.float32) / ATTN_HEAD_DIM, BF16)
    head_out = pl.BlockSpec((1, ATTN_HEADS, PROJ_ROWS, ATTN_HEAD_DIM), lambda i, j: (i, 0, j, 0))
    shape = jax.ShapeDtypeStruct((b, ATTN_HEADS, t, ATTN_HEAD_DIM), BF16)
    return pl.pallas_call(
        _qkv_kernel,
        grid=(b, t // PROJ_ROWS),
        in_specs=[pl.BlockSpec((1, PROJ_ROWS, D_MODEL), lambda i, j: (i, j, 0)),
                  _resident((1, D_MODEL)), _resident((D_MODEL, 3 * D_MODEL)),
                  _resident((1, ATTN_HEAD_DIM)), _resident((1, ATTN_HEAD_DIM)),
                  pl.BlockSpec((1, PROJ_ROWS, 1), lambda i, j: (i, j, 0)),
                  _resident((1, ATTN_HEAD_DIM)), _resident(head_mean.shape)],
        out_specs=[head_out, head_out, head_out],
        out_shape=[shape, shape, shape],
        compiler_params=_params(2),
        name="qkv_rope",
    )(x, g, w, q_gain, k_gain, pos3, inv_freq, head_mean)


def _eye(n):
    return jnp.where(lax.broadcasted_iota(jnp.int32, (n, n), 0)
                     == lax.broadcasted_iota(jnp.int32, (n, n), 1), 1.0, 0.0).astype(BF16)


def _moba_kernel(q_ref, k_ref, v_ref, o_ref, vt_ref, s_ref, p_ref):
    blk = MOBA_BLOCK
    t = q_ref.shape[2]
    n_blk = t // blk
    scale = ATTN_HEAD_DIM ** -0.5
    q_all = q_ref[0, 0]
    vt_ref[...] = _dot_nt(_eye(ATTN_HEAD_DIM), v_ref[0, 0]).astype(BF16)

    means = [jnp.mean(k_ref[0, 0, n * blk:(n + 1) * blk, :].astype(F32), axis=0, keepdims=True)
             for n in range(n_blk)]
    gate = _dot_nt(jnp.concatenate(means, axis=0).astype(BF16), q_all)
    key_blk = lax.broadcasted_iota(jnp.int32, gate.shape, 0)
    qry_blk = lax.broadcasted_iota(jnp.int32, gate.shape, 1) // blk
    past = key_blk < qry_blk
    gate = jnp.where(past, gate, -jnp.inf)
    rank = jnp.zeros(gate.shape, F32)
    for n in range(n_blk):
        other = gate[n:n + 1, :]
        rank = (rank + jnp.where(other > gate, 1.0, 0.0)
                + jnp.where(other == gate, jnp.where(key_blk > n, 1.0, 0.0), 0.0))
    bias = jnp.where(past & (rank < MOBA_TOPK), 0.0, -jnp.inf)

    c = scale * math.log2(math.e)
    causal = (lax.broadcasted_iota(jnp.int32, (blk, blk), 0)
              <= lax.broadcasted_iota(jnp.int32, (blk, blk), 1))
    eye_q = _eye(blk)

    def raw_scores(qi):
        n_keys = (qi + 1) * blk
        s_ref[qi % 2, :n_keys, :] = _dot_nt(k_ref[0, 0, :n_keys, :],
                                            q_ref[0, 0, qi * blk:n_keys, :])

    raw_scores(0)
    for qi in range(n_blk):
        if qi + 1 < n_blk:
            raw_scores(qi + 1)
        slot = qi % 2
        q_lo = qi * blk
        n_keys = q_lo + blk
        rows = lambda n: slice(n * blk, (n + 1) * blk)
        diag = jnp.where(causal, s_ref[slot, rows(qi), :], -jnp.inf)
        maxes = [jnp.max(s_ref[slot, rows(n), :], axis=0, keepdims=True)
                 + bias[n:n + 1, q_lo:n_keys] for n in range(qi)]
        maxes.append(jnp.max(diag, axis=0, keepdims=True))
        m = functools.reduce(jnp.maximum, maxes)
        l = jnp.zeros((1, blk), F32)
        for n in range(qi + 1):
            if n < qi:
                p = jnp.exp2(s_ref[slot, rows(n), :] * c + (bias[n:n + 1, q_lo:n_keys] - m) * c)
            else:
                p = jnp.exp2(diag * c - m * c)
            l = l + jnp.sum(p, axis=0, keepdims=True)
            p_ref[slot, rows(n), :] = p.astype(BF16)
        o_t = _dot(vt_ref[:, :n_keys], p_ref[slot, :n_keys, :]) / l
        o_ref[0, q_lo:n_keys, :] = _dot_nt(eye_q, o_t.astype(BF16)).astype(BF16)


def _moba(q, k, v):
    b, h, t, d = q.shape
    head = pl.BlockSpec((1, 1, t, d), lambda i, j: (i, j, 0, 0))
    return pl.pallas_call(
        _moba_kernel,
        grid=(b, h),
        in_specs=[head, head, head],
        out_specs=pl.BlockSpec((1, t, d), lambda i, j: (i, 0, j)),
        out_shape=jax.ShapeDtypeStruct((b, t, h * d), BF16),
        scratch_shapes=[pltpu.VMEM((d, t), BF16), pltpu.VMEM((2, t, MOBA_BLOCK), F32),
                        pltpu.VMEM((2, t, MOBA_BLOCK), BF16)],
        compiler_params=_params(2),
        name="moba_attn",
    )(q, k, v)


def _ssm_in_kernel(x_ref, halo_ref, g_ref, wz_ref, wx_ref, wdt_ref, cw_ref, cb_ref, dtb_ref,
                   z_ref, xs_ref, b_ref, c_ref, dt_ref, proj_ref):
    j = pl.program_id(1)
    rows = x_ref.shape[1]
    g = g_ref[...]
    n = _rms(x_ref[0], g).astype(BF16)
    n_all = jnp.concatenate([_rms(halo_ref[0], g).astype(BF16), n], axis=0)
    halo_keep = jnp.where(j > 0, 1.0, 0.0)
    sub = lax.broadcasted_iota(jnp.int32, (rows // SUBLANES, SUBLANES, SSM_IN_COLS), 1)

    def shift_rows(cur3, prev8, back):
        rot = pltpu.roll(cur3, back, axis=1)
        rot_prev = jnp.concatenate([pltpu.roll(prev8, back, axis=0)[None], rot[:-1]], axis=0)
        return jnp.where(sub >= back, rot, rot_prev)

    n_chunks = SSM_CONV_DIM // SSM_IN_COLS

    def project(c):
        proj_ref[c % 2] = _dot(n_all, wx_ref[:, c * SSM_IN_COLS:(c + 1) * SSM_IN_COLS])

    project(0)
    for c in range(n_chunks):
        cols = slice(c * SSM_IN_COLS, (c + 1) * SSM_IN_COLS)
        if c + 1 < n_chunks:
            project(c + 1)
        else:
            z_ref[0] = _dot(n, wz_ref[...])
        prev8 = proj_ref[c % 2, HALO_ROWS - SUBLANES:HALO_ROWS, :] * halo_keep
        cur3 = proj_ref[c % 2, HALO_ROWS:, :].reshape(sub.shape)
        w = [cw_ref[k:k + 1, cols] for k in range(SSM_CONV)]
        back1 = shift_rows(cur3, prev8, 1)
        near = cur3 * w[3] + back1 * w[2]
        far = cur3 * w[1] + back1 * w[0]
        far_prev8 = prev8 * w[1] + pltpu.roll(prev8, 1, axis=0) * w[0]
        y = _silu(cb_ref[:, cols] + near + shift_rows(far, far_prev8, 2)).reshape(rows, SSM_IN_COLS)
        lo = c * SSM_IN_COLS
        if lo < SSM_D_INNER:
            xs_ref[0, :, cols] = y
        elif lo < SSM_D_INNER + SSM_BC_DIM:
            b_ref[0, :, lo - SSM_D_INNER:lo - SSM_D_INNER + SSM_IN_COLS] = y.astype(BF16)
        else:
            off = lo - SSM_D_INNER - SSM_BC_DIM
            c_ref[0, :, off:off + SSM_IN_COLS] = y.astype(BF16)

    dt_raw = _dot(n, wdt_ref[...]) + dtb_ref[...]
    softplus = jnp.maximum(dt_raw, 0.0) + jnp.log1p(jnp.exp(-jnp.abs(dt_raw)))
    lane = lax.broadcasted_iota(jnp.int32, dt_raw.shape, 1)
    dt_ref[0] = jnp.where(lane < SSM_HEADS, softplus, 0.0)


def _ssm_in(x, g, wz, wx, wdt, conv_w, conv_b, dt_bias):
    b, t, _ = x.shape
    rows = SSM_IN_ROWS
    halo_blocks = rows // HALO_ROWS

    def tok(width, dtype):
        return (pl.BlockSpec((1, rows, width), lambda i, j: (i, j, 0)),
                jax.ShapeDtypeStruct((b, t, width), dtype))

    outs = [tok(SSM_D_INNER, F32), tok(SSM_D_INNER, F32), tok(SSM_BC_DIM, BF16),
            tok(SSM_BC_DIM, BF16), tok(LANES, F32)]
    return pl.pallas_call(
        _ssm_in_kernel,
        grid=(b, t // rows),
        in_specs=[pl.BlockSpec((1, rows, D_MODEL), lambda i, j: (i, j, 0)),
                  pl.BlockSpec((1, HALO_ROWS, D_MODEL),
                               lambda i, j: (i, jnp.maximum(j * halo_blocks - 1, 0), 0)),
                  _resident((1, D_MODEL)), _resident((D_MODEL, SSM_D_INNER)),
                  _resident((D_MODEL, SSM_CONV_DIM)), _resident((D_MODEL, LANES)),
                  _resident((SSM_CONV, SSM_CONV_DIM)), _resident((1, SSM_CONV_DIM)),
                  _resident((1, LANES))],
        out_specs=[o[0] for o in outs],
        out_shape=[o[1] for o in outs],
        scratch_shapes=[pltpu.VMEM((2, HALO_ROWS + rows, SSM_IN_COLS), F32)],
        compiler_params=_params(2),
        name="ssm_in",
    )(x, x, g, wz, wx, wdt, conv_w, conv_b, dt_bias)


def _ssd_kernel(xs_ref, b_ref, c_ref, dt_ref, z_ref, alog_ref, dskip_ref, ng_ref,
                tri_ref, eye_ref, ehead_ref, y_ref, state_ref, ybuf_ref, xdt_ref, xdd_ref):
    ch = SSM_CHUNK
    heads_per_group = SSM_HEADS // SSM_GROUPS
    group_w = heads_per_group * SSM_HEAD_DIM

    @pl.when(pl.program_id(1) == 0)
    def _():
        state_ref[...] = jnp.zeros(state_ref.shape, F32)

    dt = dt_ref[0]
    a = dt * (-jnp.exp(alog_ref[...]) * math.log2(math.e))
    a_cs = _dot(tri_ref[...], jnp.concatenate(_split3(a), axis=0))
    a_last = a_cs[ch - 1:ch, :]
    decay = jnp.exp2(a_last - a_cs)
    stack = jnp.concatenate([dt, decay], axis=0)
    wide = _dot(jnp.concatenate(_split2(stack), axis=1), ehead_ref[:2 * LANES, :])
    dt_x = wide[:ch]
    decay_x = wide[ch:]
    tail = jnp.concatenate([jnp.exp2(a_last), jnp.zeros((SUBLANES - 1, LANES), F32)], axis=0)
    chunk_decay_x = _dot(jnp.concatenate(_split3(tail), axis=1), ehead_ref[...])[:1]

    acs_row = _dot_nt(eye_ref[...], jnp.concatenate(_split3(a_cs), axis=1))

    x = xs_ref[0]
    xdt = x * dt_x
    xdt_ref[...] = xdt.astype(BF16)
    xdd_ref[...] = (xdt * decay_x).astype(BF16)
    tril = (lax.broadcasted_iota(jnp.int32, (ch, ch), 0)
            >= lax.broadcasted_iota(jnp.int32, (ch, ch), 1))
    lane = lax.broadcasted_iota(jnp.int32, (ch, LANES), 1)

    for g in range(SSM_GROUPS):
        bg = b_ref[0, :, g * SSM_STATE:(g + 1) * SSM_STATE]
        cg = c_ref[0, :, g * SSM_STATE:(g + 1) * SSM_STATE]
        cb = _dot_nt(cg, bg)
        bg_t = bg.T
        gl = g * group_w
        state_g = state_ref[:, gl:gl + group_w]
        lhs = []
        for r in range(heads_per_group):
            h = g * heads_per_group + r
            col = jnp.broadcast_to(a_cs[:, h:h + 1], (ch, LANES))
            seg = col - acs_row[h:h + 1, :]
            decay_ls = jnp.exp2(jnp.where(tril, seg, -jnp.inf))
            lhs.append(jnp.concatenate([(cb * decay_ls).astype(BF16),
                                        (cg.astype(F32) * jnp.exp2(col)).astype(BF16)], axis=1))
        for p in range(heads_per_group // 2):
            pl_lo = gl + p * LANES
            rhs = jnp.concatenate([xdt_ref[:, pl_lo:pl_lo + LANES],
                                   state_g[:, p * LANES:(p + 1) * LANES].astype(BF16)], axis=0)
            y0 = _dot(lhs[2 * p], rhs)
            y1 = _dot(lhs[2 * p + 1], rhs)
            ybuf_ref[:, pl_lo:pl_lo + LANES] = jnp.where(lane < SSM_HEAD_DIM, y0, y1)
        new_state = _dot(bg_t, xdd_ref[:, gl:gl + group_w])
        state_ref[:, gl:gl + group_w] = state_g * chunk_decay_x[:, gl:gl + group_w] + new_state

    z = z_ref[0]
    y = (ybuf_ref[...] + dskip_ref[...] * x) * _silu(z)
    for g in range(SSM_GROUPS):
        gl = g * group_w
        yg = y[:, gl:gl + group_w]
        ms = jnp.mean(yg * yg, axis=-1, keepdims=True)
        y_ref[0, :, gl:gl + group_w] = (yg * lax.rsqrt(ms + EPS)
                                        * ng_ref[:, gl:gl + group_w]).astype(BF16)


def _ssd_constants():
    ch = SSM_CHUNK
    tri = np.tril(np.ones((ch, ch), np.float32))
    eye = np.eye(LANES, dtype=np.float32)
    ehead = np.zeros((LANES, SSM_D_INNER), np.float32)
    for h in range(SSM_HEADS):
        ehead[h, h * SSM_HEAD_DIM:(h + 1) * SSM_HEAD_DIM] = 1.0
    as_bf16 = lambda m: jnp.asarray(m, BF16)
    return (as_bf16(np.tile(tri, (1, 3))), as_bf16(np.tile(eye, (1, 3))),
            as_bf16(np.tile(ehead, (3, 1))))


def _ssd(xs, bm, cm, dt, z, a_log, d_skip_x, norm_g):
    b, t, _ = xs.shape
    ch = SSM_CHUNK
    tri3, eye3, ehead3 = _ssd_constants()

    def tok(width):
        return pl.BlockSpec((1, ch, width), lambda i, j: (i, j, 0))

    return pl.pallas_call(
        _ssd_kernel,
        grid=(b, t // ch),
        in_specs=[tok(SSM_D_INNER), tok(SSM_BC_DIM), tok(SSM_BC_DIM), tok(LANES), tok(SSM_D_INNER),
                  _resident((1, LANES)), _resident((1, SSM_D_INNER)), _resident((1, SSM_D_INNER)),
                  _resident(tri3.shape), _resident(eye3.shape), _resident(ehead3.shape)],
        out_specs=tok(SSM_D_INNER),
        out_shape=jax.ShapeDtypeStruct((b, t, SSM_D_INNER), BF16),
        scratch_shapes=[pltpu.VMEM((SSM_STATE, SSM_D_INNER), F32),
                        pltpu.VMEM((ch, SSM_D_INNER), F32),
                        pltpu.VMEM((ch, SSM_D_INNER), BF16), pltpu.VMEM((ch, SSM_D_INNER), BF16)],
        compiler_params=_params(2),
        name="ssd",
    )(xs, bm, cm, dt, z, a_log, d_skip_x, norm_g, tri3, eye3, ehead3)


def _pad_lanes(v):
    return jnp.pad(v.astype(F32), (0, LANES - v.shape[0]))[None, :]


def kernel(x, positions, ffn1_norm, ffn1_w_gate, ffn1_w_up, ffn1_w_down, mix_norm, ffn2_norm, ffn2_w_gate, ffn2_w_up, ffn2_w_down, attn_w_qkv, attn_q_norm, attn_k_norm, attn_w_o, ssm_w_in, ssm_conv_w, ssm_conv_b, ssm_dt_bias, ssm_a_log, ssm_d, ssm_norm, ssm_w_out):
    bsz, seq, d = x.shape
    rows = bsz * seq
    half = ATTN_HEAD_DIM // 2
    inv_freq = ROPE_THETA ** (-jnp.arange(half, dtype=F32) / half)
    inv_freq = jnp.concatenate([inv_freq, inv_freq])[None, :]
    pos3 = positions[:, :, None]

    def ffn(xc, layer, norm, wg, wu, wd, mixer_out=None):
        if mixer_out is not None:
            mixer_out = (mixer_out[0].reshape(rows, -1), mixer_out[1].astype(BF16))
        out = _ffn(xc.reshape(rows, d), norm[layer][None, :], wg[layer].astype(BF16),
                   wu[layer].astype(BF16), wd[layer].astype(BF16), mixer_out)
        return out.reshape(bsz, seq, d)

    ffn1 = functools.partial(ffn, norm=ffn1_norm, wg=ffn1_w_gate, wu=ffn1_w_up, wd=ffn1_w_down)
    ffn2 = functools.partial(ffn, norm=ffn2_norm, wg=ffn2_w_gate, wu=ffn2_w_up, wd=ffn2_w_down)

    x = ffn1(x, 0)
    q, k, v = _qkv(x, mix_norm[0][None, :], attn_w_qkv[0].astype(BF16), attn_q_norm[0][None, :],
                   attn_k_norm[0][None, :], pos3, inv_freq)
    x = ffn2(x, 0, mixer_out=(_moba(q, k, v), attn_w_o[0]))

    x = ffn1(x, 1)
    w_in = ssm_w_in[0]
    wz = w_in[:, :SSM_D_INNER].astype(BF16)
    wx = w_in[:, SSM_D_INNER:SSM_D_INNER + SSM_CONV_DIM].astype(BF16)
    wdt = jnp.pad(w_in[:, SSM_D_INNER + SSM_CONV_DIM:], ((0, 0), (0, LANES - SSM_HEADS))).astype(BF16)
    z, xs, bm, cm, dt = _ssm_in(x, mix_norm[1][None, :], wz, wx, wdt, ssm_conv_w[0],
                                ssm_conv_b[0][None, :], _pad_lanes(ssm_dt_bias[0]))
    d_skip_x = jnp.repeat(ssm_d[0].astype(F32), SSM_HEAD_DIM)[None, :]
    y = _ssd(xs, bm, cm, dt, z, _pad_lanes(ssm_a_log[0]), d_skip_x, ssm_norm[0][None, :])
    return ffn2(x, 1, mixer_out=(y, ssm_w_out[0]))
```

```python
import functools
import math

import numpy as np
import jax
import jax.numpy as jnp
from jax import lax
from jax.experimental import pallas as pl
from jax.experimental.pallas import tpu as pltpu

F32 = jnp.float32
BF16 = jnp.bfloat16

D_MODEL = 1024
D_FF = 2816
EPS = 1e-6

ATTN_HEADS = 8
ATTN_HEAD_DIM = 128
MOBA_BLOCK = 256
MOBA_TOPK = 3
ROPE_THETA = 10000.0

SSM_D_INNER = 2048
SSM_HEAD_DIM = 64
SSM_HEADS = 32
SSM_GROUPS = 8
SSM_STATE = 128
SSM_CONV = 4
SSM_CHUNK = 128
SSM_BC_DIM = SSM_GROUPS * SSM_STATE
SSM_CONV_DIM = SSM_D_INNER + 2 * SSM_BC_DIM

LANES = 128
VMEM_LIMIT_BYTES = 56 * 1024 * 1024

FFN_ROWS = 512
FFN_COLS = 256
PROJ_ROWS = 512
SSM_IN_ROWS = 256
SSD_CHUNKS_PER_STEP = 4
SUBLANES = 8
HALO_ROWS = 16


def _resident(shape):
    nd = len(shape)
    return pl.BlockSpec(shape, lambda *_: (0,) * nd, pipeline_mode=pl.Buffered(1))


def _params(n_axes):
    return pltpu.CompilerParams(dimension_semantics=("arbitrary",) * n_axes,
                                vmem_limit_bytes=VMEM_LIMIT_BYTES)


def _rms(x, g):
    ms = jnp.mean(x * x, axis=-1, keepdims=True)
    return x * lax.rsqrt(ms + EPS) * g


def _silu(x):
    return x * jax.nn.sigmoid(x)


def _dot(a, b):
    return jnp.dot(a, b, preferred_element_type=F32)


def _dot_nt(a, b):
    return lax.dot_general(a, b, (((1,), (1,)), ((), ())), preferred_element_type=F32)


def _split2(v):
    hi = v.astype(BF16)
    return hi, (v - hi.astype(F32)).astype(BF16)


def _split3(v):
    hi = v.astype(BF16)
    r1 = v - hi.astype(F32)
    mid = r1.astype(BF16)
    lo = (r1 - mid.astype(F32)).astype(BF16)
    return hi, mid, lo


def _ffn_kernel(*refs, has_mixer_out):
    if has_mixer_out:
        x_ref, a_ref, wp_ref, g_ref, wg_ref, wu_ref, wd_ref, o_ref = refs
        x = x_ref[...] + _dot(a_ref[...], wp_ref[...])
    else:
        x_ref, g_ref, wg_ref, wu_ref, wd_ref, o_ref = refs
        x = x_ref[...]
    n = _rms(x, g_ref[...]).astype(BF16)
    acc = jnp.zeros(x.shape, F32)
    for c in range(D_FF // FFN_COLS):
        sl = slice(c * FFN_COLS, (c + 1) * FFN_COLS)
        gate = _dot(n, wg_ref[:, sl])
        up = _dot(n, wu_ref[:, sl])
        h = (_silu(gate) * up).astype(BF16)
        acc = acc + _dot(h, wd_ref[sl, :])
    o_ref[...] = x + 0.5 * acc


def _ffn(x2d, g, wg, wu, wd, mixer_out=None):
    m = x2d.shape[0]
    row = pl.BlockSpec((FFN_ROWS, D_MODEL), lambda i: (i, 0))
    weights = [_resident((1, D_MODEL)), _resident((D_MODEL, D_FF)),
               _resident((D_MODEL, D_FF)), _resident((D_FF, D_MODEL))]
    if mixer_out is None:
        operands, specs = (x2d, g, wg, wu, wd), [row] + weights
    else:
        a2d, wp = mixer_out
        k = a2d.shape[1]
        operands = (x2d, a2d, wp, g, wg, wu, wd)
        specs = [row, pl.BlockSpec((FFN_ROWS, k), lambda i: (i, 0)),
                 _resident((k, D_MODEL))] + weights
    return pl.pallas_call(
        functools.partial(_ffn_kernel, has_mixer_out=mixer_out is not None),
        grid=(m // FFN_ROWS,),
        in_specs=specs,
        out_specs=row,
        out_shape=jax.ShapeDtypeStruct(x2d.shape, F32),
        compiler_params=_params(1),
        name="ffn",
    )(*operands)


def _qkv_kernel(x_ref, g_ref, w_ref, qg_ref, kg_ref, pos_ref, invf_ref, hmean_ref,
                q_ref, k_ref, v_ref):
    n = _rms(x_ref[0], g_ref[...]).astype(BF16)
    qkv = _dot(n, w_ref[...])
    half_dim = ATTN_HEAD_DIM // 2
    half_rows = pos_ref.shape[1] // 2
    pos = pos_ref[0].astype(F32)
    lane = lax.broadcasted_iota(jnp.int32, (half_rows, ATTN_HEAD_DIM), 1)
    low = lane < half_dim
    ang = jnp.where(low, pos[:half_rows], pos[half_rows:]) * invf_ref[...]

    def spread(t):
        swapped = pltpu.roll(t, half_dim, axis=1)
        return jnp.concatenate([jnp.where(low, t, swapped), jnp.where(low, swapped, t)], axis=0)

    cos = spread(jnp.cos(ang))
    sin = spread(jnp.sin(ang))
    lane = lax.broadcasted_iota(jnp.int32, cos.shape, 1)
    sin_signed = jnp.where(lane < half_dim, -sin, sin)

    def gain_tables(gain):
        return gain * cos, pltpu.roll(gain, half_dim, axis=1) * sin_signed

    def norm_rope_pair(t2, tables, out_ref, h0):
        ms = _dot(jnp.concatenate(_split2(t2 * t2), axis=1), hmean_ref[...])
        u2 = t2 * lax.rsqrt(ms + EPS)
        for i in range(2):
            u = u2[:, i * ATTN_HEAD_DIM:(i + 1) * ATTN_HEAD_DIM]
            out_ref[0, h0 + i] = (u * tables[0]
                                  + pltpu.roll(u, half_dim, axis=1) * tables[1]).astype(BF16)

    q_tables = gain_tables(qg_ref[...])
    k_tables = gain_tables(kg_ref[...])
    pair = 2 * ATTN_HEAD_DIM
    for h0 in range(0, ATTN_HEADS, 2):
        lo = h0 * ATTN_HEAD_DIM
        norm_rope_pair(qkv[:, lo:lo + pair], q_tables, q_ref, h0)
        norm_rope_pair(qkv[:, D_MODEL + lo:D_MODEL + lo + pair], k_tables, k_ref, h0)
        for h in (h0, h0 + 1):
            v_lo = 2 * D_MODEL + h * ATTN_HEAD_DIM
            v_ref[0, h] = qkv[:, v_lo:v_lo + ATTN_HEAD_DIM].astype(BF16)


def _qkv(x, g, w, q_gain, k_gain, pos3, inv_freq):
    b, t, _ = x.shape
    pair = 2 * ATTN_HEAD_DIM
    same_head = (np.arange(2 * pair)[:, None] % pair) // ATTN_HEAD_DIM == np.arange(pair)[None, :] // ATTN_HEAD_DIM
    head_mean = jnp.asarray(same_head.astype(np.float32) / ATTN_HEAD_DIM, BF16)
    head_out = pl.BlockSpec((1, ATTN_HEADS, PROJ_ROWS, ATTN_HEAD_DIM), lambda i, j: (i, 0, j, 0))
    shape = jax.ShapeDtypeStruct((b, ATTN_HEADS, t, ATTN_HEAD_DIM), BF16)
    return pl.pallas_call(
        _qkv_kernel,
        grid=(b, t // PROJ_ROWS),
        in_specs=[pl.BlockSpec((1, PROJ_ROWS, D_MODEL), lambda i, j: (i, j, 0)),
                  _resident((1, D_MODEL)), _resident((D_MODEL, 3 * D_MODEL)),
                  _resident((1, ATTN_HEAD_DIM)), _resident((1, ATTN_HEAD_DIM)),
                  pl.BlockSpec((1, PROJ_ROWS, 1), lambda i, j: (i, j, 0)),
                  _resident((1, ATTN_HEAD_DIM)), _resident(head_mean.shape)],
        out_specs=[head_out, head_out, head_out],
        out_shape=[shape, shape, shape],
        compiler_params=_params(2),
        name="qkv_rope",
    )(x, g, w, q_gain, k_gain, pos3, inv_freq, head_mean)


def _moba_kernel(q_ref, k_ref, v_ref, o_ref, vt_ref, s_ref, p_ref):
    blk = MOBA_BLOCK
    t = q_ref.shape[2]
    n_blk = t // blk
    scale = ATTN_HEAD_DIM ** -0.5
    q_all = q_ref[0, 0]
    vt_ref[...] = v_ref[0, 0].T

    means = [jnp.mean(k_ref[0, 0, n * blk:(n + 1) * blk, :].astype(F32), axis=0, keepdims=True)
             for n in range(n_blk)]
    gate = _dot_nt(jnp.concatenate(means, axis=0).astype(BF16), q_all)
    key_blk = lax.broadcasted_iota(jnp.int32, gate.shape, 0)
    qry_blk = lax.broadcasted_iota(jnp.int32, gate.shape, 1) // blk
    past = key_blk < qry_blk
    gate = jnp.where(past, gate, -jnp.inf)
    rank = jnp.zeros(gate.shape, F32)
    for n in range(n_blk):
        other = gate[n:n + 1, :]
        rank = (rank + jnp.where(other > gate, 1.0, 0.0)
                + jnp.where(other == gate, jnp.where(key_blk > n, 1.0, 0.0), 0.0))
    bias = jnp.where(past & (rank < MOBA_TOPK), 0.0, -jnp.inf)

    c = scale * math.log2(math.e)
    causal = (lax.broadcasted_iota(jnp.int32, (blk, blk), 0)
              <= lax.broadcasted_iota(jnp.int32, (blk, blk), 1))

    def raw_scores(qi):
        n_keys = (qi + 1) * blk
        s_ref[qi % 2, :n_keys, :] = _dot_nt(k_ref[0, 0, :n_keys, :],
                                            q_ref[0, 0, qi * blk:n_keys, :])

    raw_scores(0)
    for qi in range(n_blk):
        if qi + 1 < n_blk:
            raw_scores(qi + 1)
        slot = qi % 2
        q_lo = qi * blk
        n_keys = q_lo + blk
        rows = lambda n: slice(n * blk, (n + 1) * blk)
        diag = jnp.where(causal, s_ref[slot, rows(qi), :], -jnp.inf)
        maxes = [jnp.max(s_ref[slot, rows(n), :], axis=0, keepdims=True)
                 + bias[n:n + 1, q_lo:n_keys] for n in range(qi)]
        maxes.append(jnp.max(diag, axis=0, keepdims=True))
        m = functools.reduce(jnp.maximum, maxes)
        l = jnp.zeros((1, blk), F32)
        for n in range(qi + 1):
            if n < qi:
                p = jnp.exp2(s_ref[slot, rows(n), :] * c + (bias[n:n + 1, q_lo:n_keys] - m) * c)
            else:
                p = jnp.exp2(diag * c - m * c)
            l = l + jnp.sum(p, axis=0, keepdims=True)
            p_ref[slot, rows(n), :] = p.astype(BF16)
        o_t = _dot(vt_ref[:, :n_keys], p_ref[slot, :n_keys, :]) / l
        o_ref[0, q_lo:n_keys, :] = o_t.astype(BF16).T


def _moba(q, k, v):
    b, h, t, d = q.shape
    head = pl.BlockSpec((1, 1, t, d), lambda i, j: (i, j, 0, 0))
    return pl.pallas_call(
        _moba_kernel,
        grid=(b, h),
        in_specs=[head, head, head],
        out_specs=pl.BlockSpec((1, t, d), lambda i, j: (i, 0, j)),
        out_shape=jax.ShapeDtypeStruct((b, t, h * d), BF16),
        scratch_shapes=[pltpu.VMEM((d, t), BF16), pltpu.VMEM((2, t, MOBA_BLOCK), F32),
                        pltpu.VMEM((2, t, MOBA_BLOCK), BF16)],
        compiler_params=_params(2),
        name="moba_attn",
    )(q, k, v)


def _ssm_in_kernel(x_ref, halo_ref, g_ref, wz_ref, wx_ref, wdt_ref, cw_ref, cb_ref, dtb_ref,
                   z_ref, xs_ref, b_ref, c_ref, dt_ref):
    j = pl.program_id(1)
    rows = x_ref.shape[1]
    g = g_ref[...]
    n = _rms(x_ref[0], g).astype(BF16)
    n_halo = _rms(halo_ref[0], g).astype(BF16)
    halo_keep = jnp.where(j > 0, 1.0, 0.0)
    sub = lax.broadcasted_iota(jnp.int32, (rows // SUBLANES, SUBLANES, SSM_CONV_DIM), 1)

    def shift_rows(cur3, prev8, back):
        rot = pltpu.roll(cur3, back, axis=1)
        rot_prev = jnp.concatenate([pltpu.roll(prev8, back, axis=0)[None], rot[:-1]], axis=0)
        return jnp.where(sub >= back, rot, rot_prev)

    z_ref[0] = _dot(n, wz_ref[...])
    prev8 = _dot(n_halo, wx_ref[...])[HALO_ROWS - SUBLANES:] * halo_keep
    cur3 = _dot(n, wx_ref[...]).reshape(sub.shape)
    w = [cw_ref[k:k + 1, :] for k in range(SSM_CONV)]
    back1 = shift_rows(cur3, prev8, 1)
    near = cur3 * w[3] + back1 * w[2]
    far = cur3 * w[1] + back1 * w[0]
    far_prev8 = prev8 * w[1] + pltpu.roll(prev8, 1, axis=0) * w[0]
    y = _silu(cb_ref[...] + near + shift_rows(far, far_prev8, 2)).reshape(rows, SSM_CONV_DIM)
    xs_ref[0] = y[:, :SSM_D_INNER]
    b_ref[0] = y[:, SSM_D_INNER:SSM_D_INNER + SSM_BC_DIM].astype(BF16)
    c_ref[0] = y[:, SSM_D_INNER + SSM_BC_DIM:].astype(BF16)

    dt_raw = _dot(n, wdt_ref[...]) + dtb_ref[...]
    softplus = jnp.maximum(dt_raw, 0.0) + jnp.log1p(jnp.exp(-jnp.abs(dt_raw)))
    lane = lax.broadcasted_iota(jnp.int32, dt_raw.shape, 1)
    dt_ref[0] = jnp.where(lane < SSM_HEADS, softplus, 0.0)


def _ssm_in(x, g, wz, wx, wdt, conv_w, conv_b, dt_bias):
    b, t, _ = x.shape
    rows = SSM_IN_ROWS
    halo_blocks = rows // HALO_ROWS

    def tok(width, dtype):
        return (pl.BlockSpec((1, rows, width), lambda i, j: (i, j, 0)),
                jax.ShapeDtypeStruct((b, t, width), dtype))

    outs = [tok(SSM_D_INNER, F32), tok(SSM_D_INNER, F32), tok(SSM_BC_DIM, BF16),
            tok(SSM_BC_DIM, BF16), tok(LANES, F32)]
    return pl.pallas_call(
        _ssm_in_kernel,
        grid=(b, t // rows),
        in_specs=[pl.BlockSpec((1, rows, D_MODEL), lambda i, j: (i, j, 0)),
                  pl.BlockSpec((1, HALO_ROWS, D_MODEL),
                               lambda i, j: (i, jnp.maximum(j * halo_blocks - 1, 0), 0)),
                  _resident((1, D_MODEL)), _resident((D_MODEL, SSM_D_INNER)),
                  _resident((D_MODEL, SSM_CONV_DIM)), _resident((D_MODEL, LANES)),
                  _resident((SSM_CONV, SSM_CONV_DIM)), _resident((1, SSM_CONV_DIM)),
                  _resident((1, LANES))],
        out_specs=[o[0] for o in outs],
        out_shape=[o[1] for o in outs],
        compiler_params=_params(2),
        name="ssm_in",
    )(x, x, g, wz, wx, wdt, conv_w, conv_b, dt_bias)


def _ssd_kernel(xs_ref, b_ref, c_ref, dt_ref, z_ref, alog_ref, dskip_ref, ng_ref,
                tri_ref, ehead_ref, y_ref, state_ref, ybuf_ref, xdt_ref, xdd_ref):
    @pl.when(pl.program_id(1) == 0)
    def _():
        state_ref[...] = jnp.zeros(state_ref.shape, F32)

    for k in range(SSD_CHUNKS_PER_STEP):
        rows = slice(k * SSM_CHUNK, (k + 1) * SSM_CHUNK)
        _ssd_chunk(xs_ref.at[0, rows], b_ref.at[0, rows], c_ref.at[0, rows], dt_ref.at[0, rows],
                   z_ref.at[0, rows], alog_ref, dskip_ref, ng_ref, tri_ref, ehead_ref,
                   y_ref.at[0, rows], state_ref, ybuf_ref.at[k], xdt_ref.at[k], xdd_ref.at[k])


def _ssd_chunk(xs_ref, b_ref, c_ref, dt_ref, z_ref, alog_ref, dskip_ref, ng_ref, tri_ref, ehead_ref,
               y_ref, state_ref, ybuf_ref, xdt_ref, xdd_ref):
    ch = SSM_CHUNK
    heads_per_group = SSM_HEADS // SSM_GROUPS
    group_w = heads_per_group * SSM_HEAD_DIM
    dt = dt_ref[...]
    a = dt * (-jnp.exp(alog_ref[...]) * math.log2(math.e))
    a_cs = _dot(tri_ref[...], jnp.concatenate(_split3(a), axis=0))
    a_last = a_cs[ch - 1:ch, :]
    decay = jnp.exp2(a_last - a_cs)
    stack = jnp.concatenate([dt, decay], axis=0)
    wide = _dot(jnp.concatenate(_split2(stack), axis=1), ehead_ref[:2 * LANES, :])
    dt_x = wide[:ch]
    decay_x = wide[ch:]
    tail = jnp.concatenate([jnp.exp2(a_last), jnp.zeros((SUBLANES - 1, LANES), F32)], axis=0)
    chunk_decay_x = _dot(jnp.concatenate(_split3(tail), axis=1), ehead_ref[...])[:1]

    acs_row = a_cs.T

    x = xs_ref[...]
    xdt = x * dt_x
    xdt_ref[...] = xdt.astype(BF16)
    xdd_ref[...] = (xdt * decay_x).astype(BF16)
    tril = (lax.broadcasted_iota(jnp.int32, (ch, ch), 0)
            >= lax.broadcasted_iota(jnp.int32, (ch, ch), 1))
    lane = lax.broadcasted_iota(jnp.int32, (ch, LANES), 1)

    for g in range(SSM_GROUPS):
        bg = b_ref[:, g * SSM_STATE:(g + 1) * SSM_STATE]
        cg = c_ref[:, g * SSM_STATE:(g + 1) * SSM_STATE]
        cb = _dot_nt(cg, bg)
        bg_t = bg.T
        gl = g * group_w
        state_g = state_ref[:, gl:gl + group_w]
        lhs = []
        for r in range(heads_per_group):
            h = g * heads_per_group + r
            col = jnp.broadcast_to(a_cs[:, h:h + 1], (ch, LANES))
            seg = col - acs_row[h:h + 1, :]
            decay_ls = jnp.exp2(jnp.where(tril, seg, -jnp.inf))
            lhs.append(jnp.concatenate([(cb * decay_ls).astype(BF16),
                                        (cg.astype(F32) * jnp.exp2(col)).astype(BF16)], axis=1))
        for p in range(heads_per_group // 2):
            pl_lo = gl + p * LANES
            rhs = jnp.concatenate([xdt_ref[:, pl_lo:pl_lo + LANES],
                                   state_g[:, p * LANES:(p + 1) * LANES].astype(BF16)], axis=0)
            y0 = _dot(lhs[2 * p], rhs)
            y1 = _dot(lhs[2 * p + 1], rhs)
            ybuf_ref[:, pl_lo:pl_lo + LANES] = jnp.where(lane < SSM_HEAD_DIM, y0, y1)
        new_state = _dot(bg_t, xdd_ref[:, gl:gl + group_w])
        state_ref[:, gl:gl + group_w] = state_g * chunk_decay_x[:, gl:gl + group_w] + new_state

    z = z_ref[...]
    y = (ybuf_ref[...] + dskip_ref[...] * x) * _silu(z)
    for g in range(SSM_GROUPS):
        gl = g * group_w
        yg = y[:, gl:gl + group_w]
        ms = jnp.mean(yg * yg, axis=-1, keepdims=True)
        y_ref[:, gl:gl + group_w] = (yg * lax.rsqrt(ms + EPS)
                                     * ng_ref[:, gl:gl + group_w]).astype(BF16)


def _ssd_constants():
    ch = SSM_CHUNK
    tri = np.tril(np.ones((ch, ch), np.float32))
    ehead = np.zeros((LANES, SSM_D_INNER), np.float32)
    for h in range(SSM_HEADS):
        ehead[h, h * SSM_HEAD_DIM:(h + 1) * SSM_HEAD_DIM] = 1.0
    as_bf16 = lambda m: jnp.asarray(m, BF16)
    return as_bf16(np.tile(tri, (1, 3))), as_bf16(np.tile(ehead, (3, 1)))


def _ssd(xs, bm, cm, dt, z, a_log, d_skip_x, norm_g):
    b, t, _ = xs.shape
    rows = SSD_CHUNKS_PER_STEP * SSM_CHUNK
    tri3, ehead3 = _ssd_constants()

    def tok(width):
        return pl.BlockSpec((1, rows, width), lambda i, j: (i, j, 0))

    return pl.pallas_call(
        _ssd_kernel,
        grid=(b, t // rows),
        in_specs=[tok(SSM_D_INNER), tok(SSM_BC_DIM), tok(SSM_BC_DIM), tok(LANES), tok(SSM_D_INNER),
                  _resident((1, LANES)), _resident((1, SSM_D_INNER)), _resident((1, SSM_D_INNER)),
                  _resident(tri3.shape), _resident(ehead3.shape)],
        out_specs=tok(SSM_D_INNER),
        out_shape=jax.ShapeDtypeStruct((b, t, SSM_D_INNER), BF16),
        scratch_shapes=[pltpu.VMEM((SSM_STATE, SSM_D_INNER), F32),
                        pltpu.VMEM((SSD_CHUNKS_PER_STEP, SSM_CHUNK, SSM_D_INNER), F32),
                        pltpu.VMEM((SSD_CHUNKS_PER_STEP, SSM_CHUNK, SSM_D_INNER), BF16),
                        pltpu.VMEM((SSD_CHUNKS_PER_STEP, SSM_CHUNK, SSM_D_INNER), BF16)],
        compiler_params=_params(2),
        name="ssd",
    )(xs, bm, cm, dt, z, a_log, d_skip_x, norm_g, tri3, ehead3)


def _pad_lanes(v):
    return jnp.pad(v.astype(F32), (0, LANES - v.shape[0]))[None, :]


def kernel(x, positions, ffn1_norm, ffn1_w_gate, ffn1_w_up, ffn1_w_down, mix_norm, ffn2_norm, ffn2_w_gate, ffn2_w_up, ffn2_w_down, attn_w_qkv, attn_q_norm, attn_k_norm, attn_w_o, ssm_w_in, ssm_conv_w, ssm_conv_b, ssm_dt_bias, ssm_a_log, ssm_d, ssm_norm, ssm_w_out):
    bsz, seq, d = x.shape
    rows = bsz * seq
    half = ATTN_HEAD_DIM // 2
    inv_freq = ROPE_THETA ** (-jnp.arange(half, dtype=F32) / half)
    inv_freq = jnp.concatenate([inv_freq, inv_freq])[None, :]
    pos3 = positions[:, :, None]

    def ffn(xc, layer, norm, wg, wu, wd, mixer_out=None):
        if mixer_out is not None:
            mixer_out = (mixer_out[0].reshape(rows, -1), mixer_out[1].astype(BF16))
        out = _ffn(xc.reshape(rows, d), norm[layer][None, :], wg[layer].astype(BF16),
                   wu[layer].astype(BF16), wd[layer].astype(BF16), mixer_out)
        return out.reshape(bsz, seq, d)

    ffn1 = functools.partial(ffn, norm=ffn1_norm, wg=ffn1_w_gate, wu=ffn1_w_up, wd=ffn1_w_down)
    ffn2 = functools.partial(ffn, norm=ffn2_norm, wg=ffn2_w_gate, wu=ffn2_w_up, wd=ffn2_w_down)

    x = ffn1(x, 0)
    q, k, v = _qkv(x, mix_norm[0][None, :], attn_w_qkv[0].astype(BF16), attn_q_norm[0][None, :],
                   attn_k_norm[0][None, :], pos3, inv_freq)
    x = ffn2(x, 0, mixer_out=(_moba(q, k, v), attn_w_o[0]))

    x = ffn1(x, 1)
    w_in = ssm_w_in[0]
    wz = w_in[:, :SSM_D_INNER].astype(BF16)
    wx = w_in[:, SSM_D_INNER:SSM_D_INNER + SSM_CONV_DIM].astype(BF16)
    wdt = jnp.pad(w_in[:, SSM_D_INNER + SSM_CONV_DIM:], ((0, 0), (0, LANES - SSM_HEADS))).astype(BF16)
    z, xs, bm, cm, dt = _ssm_in(x, mix_norm[1][None, :], wz, wx, wdt, ssm_conv_w[0],
                                ssm_conv_b[0][None, :], _pad_lanes(ssm_dt_bias[0]))
    d_skip_x = jnp.repeat(ssm_d[0].astype(F32), SSM_HEAD_DIM)[None, :]
    y = _ssd(xs, bm, cm, dt, z, _pad_lanes(ssm_a_log[0]), d_skip_x, ssm_norm[0][None, :])
    return ffn2(x, 1, mixer_out=(y, ssm_w_out[0]))
```

```python
import functools
import math

import numpy as np
import jax
import jax.numpy as jnp
from jax import lax
from jax.experimental import pallas as pl
from jax.experimental.pallas import tpu as pltpu

F32 = jnp.float32
BF16 = jnp.bfloat16

D_MODEL = 1024
D_FF = 2816
EPS = 1e-6

ATTN_HEADS = 8
ATTN_HEAD_DIM = 128
MOBA_BLOCK = 256
MOBA_TOPK = 3
ROPE_THETA = 10000.0

SSM_D_INNER = 2048
SSM_HEAD_DIM = 64
SSM_HEADS = 32
SSM_GROUPS = 8
SSM_STATE = 128
SSM_CONV = 4
SSM_CHUNK = 128
SSM_BC_DIM = SSM_GROUPS * SSM_STATE
SSM_CONV_DIM = SSM_D_INNER + 2 * SSM_BC_DIM

LANES = 128
VMEM_LIMIT_BYTES = 56 * 1024 * 1024

FFN_ROWS = 512
FFN_COLS = 256
PROJ_ROWS = 512
SSM_IN_ROWS = 256
SSD_CHUNKS_PER_STEP = 4
SUBLANES = 8
HALO_ROWS = 16


def _resident(shape):
    nd = len(shape)
    return pl.BlockSpec(shape, lambda *_: (0,) * nd, pipeline_mode=pl.Buffered(1))


def _params(n_axes):
    return pltpu.CompilerParams(dimension_semantics=("arbitrary",) * n_axes,
                                vmem_limit_bytes=VMEM_LIMIT_BYTES)


def _rms(x, g):
    ms = jnp.mean(x * x, axis=-1, keepdims=True)
    return x * lax.rsqrt(ms + EPS) * g


def _silu(x):
    return x * jax.nn.sigmoid(x)


def _dot(a, b):
    return jnp.dot(a, b, preferred_element_type=F32)


def _dot_nt(a, b):
    return lax.dot_general(a, b, (((1,), (1,)), ((), ())), preferred_element_type=F32)


def _split2(v):
    hi = v.astype(BF16)
    return hi, (v - hi.astype(F32)).astype(BF16)


def _split3(v):
    hi = v.astype(BF16)
    r1 = v - hi.astype(F32)
    mid = r1.astype(BF16)
    lo = (r1 - mid.astype(F32)).astype(BF16)
    return hi, mid, lo


def _ffn_kernel(*refs, has_mixer_out):
    if has_mixer_out:
        x_ref, a_ref, wp_ref, g_ref, wg_ref, wu_ref, wd_ref, o_ref = refs
        x = x_ref[...] + _dot(a_ref[...], wp_ref[...])
    else:
        x_ref, g_ref, wg_ref, wu_ref, wd_ref, o_ref = refs
        x = x_ref[...]
    n = _rms(x, g_ref[...]).astype(BF16)
    acc = jnp.zeros(x.shape, F32)
    for c in range(D_FF // FFN_COLS):
        sl = slice(c * FFN_COLS, (c + 1) * FFN_COLS)
        gate = _dot(n, wg_ref[:, sl])
        up = _dot(n, wu_ref[:, sl])
        h = (_silu(gate) * up).astype(BF16)
        acc = acc + _dot(h, wd_ref[sl, :])
    o_ref[...] = x + 0.5 * acc


def _ffn(x2d, g, wg, wu, wd, mixer_out=None):
    m = x2d.shape[0]
    row = pl.BlockSpec((FFN_ROWS, D_MODEL), lambda i: (i, 0))
    weights = [_resident((1, D_MODEL)), _resident((D_MODEL, D_FF)),
               _resident((D_MODEL, D_FF)), _resident((D_FF, D_MODEL))]
    if mixer_out is None:
        operands, specs = (x2d, g, wg, wu, wd), [row] + weights
    else:
        a2d, wp = mixer_out
        k = a2d.shape[1]
        operands = (x2d, a2d, wp, g, wg, wu, wd)
        specs = [row, pl.BlockSpec((FFN_ROWS, k), lambda i: (i, 0)),
                 _resident((k, D_MODEL))] + weights
    return pl.pallas_call(
        functools.partial(_ffn_kernel, has_mixer_out=mixer_out is not None),
        grid=(m // FFN_ROWS,),
        in_specs=specs,
        out_specs=row,
        out_shape=jax.ShapeDtypeStruct(x2d.shape, F32),
        compiler_params=_params(1),
        name="ffn",
    )(*operands)


def _qkv_kernel(x_ref, g_ref, w_ref, qg_ref, kg_ref, pos_ref, invf_ref, hmean_ref,
                q_ref, k_ref, v_ref):
    n = _rms(x_ref[0], g_ref[...]).astype(BF16)
    qkv = _dot(n, w_ref[...])
    half_dim = ATTN_HEAD_DIM // 2
    half_rows = pos_ref.shape[1] // 2
    pos = pos_ref[0].astype(F32)
    lane = lax.broadcasted_iota(jnp.int32, (half_rows, ATTN_HEAD_DIM), 1)
    low = lane < half_dim
    ang = jnp.where(low, pos[:half_rows], pos[half_rows:]) * invf_ref[...]

    def spread(t):
        swapped = pltpu.roll(t, half_dim, axis=1)
        return jnp.concatenate([jnp.where(low, t, swapped), jnp.where(low, swapped, t)], axis=0)

    cos = spread(jnp.cos(ang))
    sin = spread(jnp.sin(ang))
    lane = lax.broadcasted_iota(jnp.int32, cos.shape, 1)
    sin_signed = jnp.where(lane < half_dim, -sin, sin)

    def gain_tables(gain):
        return gain * cos, pltpu.roll(gain, half_dim, axis=1) * sin_signed

    def norm_rope_pair(t2, tables, out_ref, h0):
        ms = _dot(jnp.concatenate(_split2(t2 * t2), axis=1), hmean_ref[...])
        u2 = t2 * lax.rsqrt(ms + EPS)
        for i in range(2):
            u = u2[:, i * ATTN_HEAD_DIM:(i + 1) * ATTN_HEAD_DIM]
            out_ref[0, h0 + i] = (u * tables[0]
                                  + pltpu.roll(u, half_dim, axis=1) * tables[1]).astype(BF16)

    q_tables = gain_tables(qg_ref[...] * (ATTN_HEAD_DIM ** -0.5 * math.log2(math.e)))
    k_tables = gain_tables(kg_ref[...])
    pair = 2 * ATTN_HEAD_DIM
    for h0 in range(0, ATTN_HEADS, 2):
        lo = h0 * ATTN_HEAD_DIM
        norm_rope_pair(qkv[:, lo:lo + pair], q_tables, q_ref, h0)
        norm_rope_pair(qkv[:, D_MODEL + lo:D_MODEL + lo + pair], k_tables, k_ref, h0)
        for h in (h0, h0 + 1):
            v_lo = 2 * D_MODEL + h * ATTN_HEAD_DIM
            v_ref[0, h] = qkv[:, v_lo:v_lo + ATTN_HEAD_DIM].astype(BF16)


def _qkv(x, g, w, q_gain, k_gain, pos3, inv_freq):
    b, t, _ = x.shape
    pair = 2 * ATTN_HEAD_DIM
    same_head = (np.arange(2 * pair)[:, None] % pair) // ATTN_HEAD_DIM == np.arange(pair)[None, :] // ATTN_HEAD_DIM
    head_mean = jnp.asarray(same_head.astype(np.float32) / ATTN_HEAD_DIM, BF16)
    head_out = pl.BlockSpec((1, ATTN_HEADS, PROJ_ROWS, ATTN_HEAD_DIM), lambda i, j: (i, 0, j, 0))
    shape = jax.ShapeDtypeStruct((b, ATTN_HEADS, t, ATTN_HEAD_DIM), BF16)
    return pl.pallas_call(
        _qkv_kernel,
        grid=(b, t // PROJ_ROWS),
        in_specs=[pl.BlockSpec((1, PROJ_ROWS, D_MODEL), lambda i, j: (i, j, 0)),
                  _resident((1, D_MODEL)), _resident((D_MODEL, 3 * D_MODEL)),
                  _resident((1, ATTN_HEAD_DIM)), _resident((1, ATTN_HEAD_DIM)),
                  pl.BlockSpec((1, PROJ_ROWS, 1), lambda i, j: (i, j, 0)),
                  _resident((1, ATTN_HEAD_DIM)), _resident(head_mean.shape)],
        out_specs=[head_out, head_out, head_out],
        out_shape=[shape, shape, shape],
        compiler_params=_params(2),
        name="qkv_rope",
    )(x, g, w, q_gain, k_gain, pos3, inv_freq, head_mean)


def _moba_kernel(q_ref, k_ref, v_ref, o_ref, vt_ref, s_ref, p_ref):
    blk = MOBA_BLOCK
    t = q_ref.shape[2]
    n_blk = t // blk
    q_all = q_ref[0, 0]
    vt_ref[...] = v_ref[0, 0].T

    means = [jnp.mean(k_ref[0, 0, n * blk:(n + 1) * blk, :].astype(F32), axis=0, keepdims=True)
             for n in range(n_blk)]
    gate = _dot_nt(jnp.concatenate(means, axis=0).astype(BF16), q_all)
    key_blk = lax.broadcasted_iota(jnp.int32, gate.shape, 0)
    qry_blk = lax.broadcasted_iota(jnp.int32, gate.shape, 1) // blk
    past = key_blk < qry_blk
    gate = jnp.where(past, gate, -jnp.inf)
    rank = jnp.zeros(gate.shape, F32)
    for n in range(n_blk):
        other = gate[n:n + 1, :]
        rank = (rank + jnp.where(other > gate, 1.0, 0.0)
                + jnp.where(other == gate, jnp.where(key_blk > n, 1.0, 0.0), 0.0))
    bias = jnp.where(past & (rank < MOBA_TOPK), 0.0, -jnp.inf)

    causal = (lax.broadcasted_iota(jnp.int32, (blk, blk), 0)
              <= lax.broadcasted_iota(jnp.int32, (blk, blk), 1))

    def raw_scores(qi):
        n_keys = (qi + 1) * blk
        s_ref[qi % 2, :n_keys, :] = _dot_nt(k_ref[0, 0, :n_keys, :],
                                            q_ref[0, 0, qi * blk:n_keys, :])

    raw_scores(0)
    for qi in range(n_blk):
        if qi + 1 < n_blk:
            raw_scores(qi + 1)
        slot = qi % 2
        q_lo = qi * blk
        n_keys = q_lo + blk
        rows = lambda n: slice(n * blk, (n + 1) * blk)
        diag = jnp.where(causal, s_ref[slot, rows(qi), :], -jnp.inf)
        maxes = [jnp.max(s_ref[slot, rows(n), :], axis=0, keepdims=True)
                 + bias[n:n + 1, q_lo:n_keys] for n in range(qi)]
        maxes.append(jnp.max(diag, axis=0, keepdims=True))
        m = functools.reduce(jnp.maximum, maxes)
        l = jnp.zeros((1, blk), F32)
        for n in range(qi + 1):
            if n < qi:
                p = jnp.exp2(s_ref[slot, rows(n), :] + (bias[n:n + 1, q_lo:n_keys] - m))
            else:
                p = jnp.exp2(diag - m)
            l = l + jnp.sum(p, axis=0, keepdims=True)
            p_ref[slot, rows(n), :] = p.astype(BF16)
        o_t = _dot(vt_ref[:, :n_keys], p_ref[slot, :n_keys, :]) / l
        o_ref[0, q_lo:n_keys, :] = o_t.astype(BF16).T


def _moba(q, k, v):
    b, h, t, d = q.shape
    head = pl.BlockSpec((1, 1, t, d), lambda i, j: (i, j, 0, 0))
    return pl.pallas_call(
        _moba_kernel,
        grid=(b, h),
        in_specs=[head, head, head],
        out_specs=pl.BlockSpec((1, t, d), lambda i, j: (i, 0, j)),
        out_shape=jax.ShapeDtypeStruct((b, t, h * d), BF16),
        scratch_shapes=[pltpu.VMEM((d, t), BF16), pltpu.VMEM((2, t, MOBA_BLOCK), F32),
                        pltpu.VMEM((2, t, MOBA_BLOCK), BF16)],
        compiler_params=_params(2),
        name="moba_attn",
    )(q, k, v)


def _ssm_in_kernel(x_ref, halo_ref, g_ref, wz_ref, wx_ref, wdt_ref, cw_ref, cb_ref, dtb_ref,
                   z_ref, xs_ref, b_ref, c_ref, dt_ref):
    j = pl.program_id(1)
    rows = x_ref.shape[1]
    g = g_ref[...]
    n = _rms(x_ref[0], g).astype(BF16)
    n_halo = _rms(halo_ref[0], g).astype(BF16)
    halo_keep = jnp.where(j > 0, 1.0, 0.0)
    sub = lax.broadcasted_iota(jnp.int32, (rows // SUBLANES, SUBLANES, SSM_CONV_DIM), 1)

    def shift_rows(cur3, prev8, back):
        rot = pltpu.roll(cur3, back, axis=1)
        rot_prev = jnp.concatenate([pltpu.roll(prev8, back, axis=0)[None], rot[:-1]], axis=0)
        return jnp.where(sub >= back, rot, rot_prev)

    z_ref[0] = _dot(n, wz_ref[...])
    prev8 = _dot(n_halo, wx_ref[...])[HALO_ROWS - SUBLANES:] * halo_keep
    cur3 = _dot(n, wx_ref[...]).reshape(sub.shape)
    w = [cw_ref[k:k + 1, :] for k in range(SSM_CONV)]
    back1 = shift_rows(cur3, prev8, 1)
    near = cur3 * w[3] + back1 * w[2]
    far = cur3 * w[1] + back1 * w[0]
    far_prev8 = prev8 * w[1] + pltpu.roll(prev8, 1, axis=0) * w[0]
    y = _silu(cb_ref[...] + near + shift_rows(far, far_prev8, 2)).reshape(rows, SSM_CONV_DIM)
    xs_ref[0] = y[:, :SSM_D_INNER]
    b_ref[0] = y[:, SSM_D_INNER:SSM_D_INNER + SSM_BC_DIM].astype(BF16)
    c_ref[0] = y[:, SSM_D_INNER + SSM_BC_DIM:].astype(BF16)

    dt_raw = _dot(n, wdt_ref[...]) + dtb_ref[...]
    softplus = jnp.maximum(dt_raw, 0.0) + jnp.log1p(jnp.exp(-jnp.abs(dt_raw)))
    lane = lax.broadcasted_iota(jnp.int32, dt_raw.shape, 1)
    dt_ref[0] = jnp.where(lane < SSM_HEADS, softplus, 0.0)


def _ssm_in(x, g, wz, wx, wdt, conv_w, conv_b, dt_bias):
    b, t, _ = x.shape
    rows = SSM_IN_ROWS
    halo_blocks = rows // HALO_ROWS

    def tok(width, dtype):
        return (pl.BlockSpec((1, rows, width), lambda i, j: (i, j, 0)),
                jax.ShapeDtypeStruct((b, t, width), dtype))

    outs = [tok(SSM_D_INNER, F32), tok(SSM_D_INNER, F32), tok(SSM_BC_DIM, BF16),
            tok(SSM_BC_DIM, BF16), tok(LANES, F32)]
    return pl.pallas_call(
        _ssm_in_kernel,
        grid=(b, t // rows),
        in_specs=[pl.BlockSpec((1, rows, D_MODEL), lambda i, j: (i, j, 0)),
                  pl.BlockSpec((1, HALO_ROWS, D_MODEL),
                               lambda i, j: (i, jnp.maximum(j * halo_blocks - 1, 0), 0)),
                  _resident((1, D_MODEL)), _resident((D_MODEL, SSM_D_INNER)),
                  _resident((D_MODEL, SSM_CONV_DIM)), _resident((D_MODEL, LANES)),
                  _resident((SSM_CONV, SSM_CONV_DIM)), _resident((1, SSM_CONV_DIM)),
                  _resident((1, LANES))],
        out_specs=[o[0] for o in outs],
        out_shape=[o[1] for o in outs],
        compiler_params=_params(2),
        name="ssm_in",
    )(x, x, g, wz, wx, wdt, conv_w, conv_b, dt_bias)


def _ssd_kernel(xs_ref, b_ref, c_ref, dt_ref, z_ref, alog_ref, dskip_ref, ng_ref,
                tri_ref, ehead_ref, y_ref, state_ref, ybuf_ref, xdt_ref, xdd_ref):
    @pl.when(pl.program_id(1) == 0)
    def _():
        state_ref[...] = jnp.zeros(state_ref.shape, F32)

    for k in range(SSD_CHUNKS_PER_STEP):
        rows = slice(k * SSM_CHUNK, (k + 1) * SSM_CHUNK)
        _ssd_chunk(xs_ref.at[0, rows], b_ref.at[0, rows], c_ref.at[0, rows], dt_ref.at[0, rows],
                   z_ref.at[0, rows], alog_ref, dskip_ref, ng_ref, tri_ref, ehead_ref,
                   y_ref.at[0, rows], state_ref, ybuf_ref.at[k], xdt_ref.at[k], xdd_ref.at[k])


def _ssd_chunk(xs_ref, b_ref, c_ref, dt_ref, z_ref, alog_ref, dskip_ref, ng_ref, tri_ref, ehead_ref,
               y_ref, state_ref, ybuf_ref, xdt_ref, xdd_ref):
    ch = SSM_CHUNK
    heads_per_group = SSM_HEADS // SSM_GROUPS
    group_w = heads_per_group * SSM_HEAD_DIM
    dt = dt_ref[...]
    a = dt * (-jnp.exp(alog_ref[...]) * math.log2(math.e))
    a_cs = _dot(tri_ref[...], jnp.concatenate(_split3(a), axis=0))
    a_last = a_cs[ch - 1:ch, :]
    decay = jnp.exp2(a_last - a_cs)
    stack = jnp.concatenate([dt, decay], axis=0)
    wide = _dot(jnp.concatenate(_split2(stack), axis=1), ehead_ref[:2 * LANES, :])
    dt_x = wide[:ch]
    decay_x = wide[ch:]
    tail = jnp.concatenate([jnp.exp2(a_last), jnp.zeros((SUBLANES - 1, LANES), F32)], axis=0)
    chunk_decay_x = _dot(jnp.concatenate(_split3(tail), axis=1), ehead_ref[...])[:1]

    acs_row = a_cs.T

    x = xs_ref[...]
    xdt = x * dt_x
    xdt_b = xdt.astype(BF16)
    xdd_b = (xdt * decay_x).astype(BF16)
    for p in range(SSM_HEADS // 2):
        xdt_ref[p] = xdt_b[:, p * LANES:(p + 1) * LANES]
    for g in range(SSM_GROUPS):
        xdd_ref[g] = xdd_b[:, g * group_w:(g + 1) * group_w]
    tril = (lax.broadcasted_iota(jnp.int32, (ch, ch), 0)
            >= lax.broadcasted_iota(jnp.int32, (ch, ch), 1))
    lane = lax.broadcasted_iota(jnp.int32, (ch, LANES), 1)

    for g in range(SSM_GROUPS):
        bg = b_ref[:, g * SSM_STATE:(g + 1) * SSM_STATE]
        cg = c_ref[:, g * SSM_STATE:(g + 1) * SSM_STATE]
        cb = _dot_nt(cg, bg)
        bg_t = bg.T
        gl = g * group_w
        state_g = state_ref[:, gl:gl + group_w]
        lhs = []
        for r in range(heads_per_group):
            h = g * heads_per_group + r
            row = jnp.broadcast_to(acs_row[h:h + 1, :], (ch, LANES))
            col = row.T
            seg = col - row
            decay_ls = jnp.exp2(jnp.where(tril, seg, -jnp.inf))
            lhs.append(jnp.concatenate([(cb * decay_ls).astype(BF16),
                                        (cg.astype(F32) * jnp.exp2(col)).astype(BF16)], axis=1))
        for p in range(heads_per_group // 2):
            pair = g * (heads_per_group // 2) + p
            rhs = jnp.concatenate([xdt_ref[pair],
                                   state_g[:, p * LANES:(p + 1) * LANES].astype(BF16)], axis=0)
            y0 = _dot(lhs[2 * p], rhs)
            y1 = _dot(lhs[2 * p + 1], rhs)
            ybuf_ref[pair] = jnp.where(lane < SSM_HEAD_DIM, y0, y1)
        new_state = _dot(bg_t, xdd_ref[g])
        state_ref[:, gl:gl + group_w] = state_g * chunk_decay_x[:, gl:gl + group_w] + new_state

    z = z_ref[...]
    y_ssd = jnp.concatenate([ybuf_ref[p] for p in range(SSM_HEADS // 2)], axis=1)
    y = (y_ssd + dskip_ref[...] * x) * _silu(z)
    for g in range(SSM_GROUPS):
        gl = g * group_w
        yg = y[:, gl:gl + group_w]
        ms = jnp.mean(yg * yg, axis=-1, keepdims=True)
        y_ref[:, gl:gl + group_w] = (yg * lax.rsqrt(ms + EPS)
                                     * ng_ref[:, gl:gl + group_w]).astype(BF16)


def _ssd_constants():
    ch = SSM_CHUNK
    tri = np.tril(np.ones((ch, ch), np.float32))
    ehead = np.zeros((LANES, SSM_D_INNER), np.float32)
    for h in range(SSM_HEADS):
        ehead[h, h * SSM_HEAD_DIM:(h + 1) * SSM_HEAD_DIM] = 1.0
    as_bf16 = lambda m: jnp.asarray(m, BF16)
    return as_bf16(np.tile(tri, (1, 3))), as_bf16(np.tile(ehead, (3, 1)))


def _ssd(xs, bm, cm, dt, z, a_log, d_skip_x, norm_g):
    b, t, _ = xs.shape
    rows = SSD_CHUNKS_PER_STEP * SSM_CHUNK
    tri3, ehead3 = _ssd_constants()

    def tok(width):
        return pl.BlockSpec((1, rows, width), lambda i, j: (i, j, 0))

    return pl.pallas_call(
        _ssd_kernel,
        grid=(b, t // rows),
        in_specs=[tok(SSM_D_INNER), tok(SSM_BC_DIM), tok(SSM_BC_DIM), tok(LANES), tok(SSM_D_INNER),
                  _resident((1, LANES)), _resident((1, SSM_D_INNER)), _resident((1, SSM_D_INNER)),
                  _resident(tri3.shape), _resident(ehead3.shape)],
        out_specs=tok(SSM_D_INNER),
        out_shape=jax.ShapeDtypeStruct((b, t, SSM_D_INNER), BF16),
        scratch_shapes=[pltpu.VMEM((SSM_STATE, SSM_D_INNER), F32),
                        pltpu.VMEM((SSD_CHUNKS_PER_STEP, SSM_HEADS // 2, SSM_CHUNK, LANES), F32),
                        pltpu.VMEM((SSD_CHUNKS_PER_STEP, SSM_HEADS // 2, SSM_CHUNK, LANES), BF16),
                        pltpu.VMEM((SSD_CHUNKS_PER_STEP, SSM_GROUPS, SSM_CHUNK, 2 * LANES), BF16)],
        compiler_params=_params(2),
        name="ssd",
    )(xs, bm, cm, dt, z, a_log, d_skip_x, norm_g, tri3, ehead3)


def _pad_lanes(v):
    return jnp.pad(v.astype(F32), (0, LANES - v.shape[0]))[None, :]


def kernel(x, positions, ffn1_norm, ffn1_w_gate, ffn1_w_up, ffn1_w_down, mix_norm, ffn2_norm, ffn2_w_gate, ffn2_w_up, ffn2_w_down, attn_w_qkv, attn_q_norm, attn_k_norm, attn_w_o, ssm_w_in, ssm_conv_w, ssm_conv_b, ssm_dt_bias, ssm_a_log, ssm_d, ssm_norm, ssm_w_out):
    bsz, seq, d = x.shape
    rows = bsz * seq
    half = ATTN_HEAD_DIM // 2
    inv_freq = ROPE_THETA ** (-jnp.arange(half, dtype=F32) / half)
    inv_freq = jnp.concatenate([inv_freq, inv_freq])[None, :]
    pos3 = positions[:, :, None]

    def ffn(xc, layer, norm, wg, wu, wd, mixer_out=None):
        if mixer_out is not None:
            mixer_out = (mixer_out[0].reshape(rows, -1), mixer_out[1].astype(BF16))
        out = _ffn(xc.reshape(rows, d), norm[layer][None, :], wg[layer].astype(BF16),
                   wu[layer].astype(BF16), wd[layer].astype(BF16), mixer_out)
        return out.reshape(bsz, seq, d)

    ffn1 = functools.partial(ffn, norm=ffn1_norm, wg=ffn1_w_gate, wu=ffn1_w_up, wd=ffn1_w_down)
    ffn2 = functools.partial(ffn, norm=ffn2_norm, wg=ffn2_w_gate, wu=ffn2_w_up, wd=ffn2_w_down)

    x = ffn1(x, 0)
    q, k, v = _qkv(x, mix_norm[0][None, :], attn_w_qkv[0].astype(BF16), attn_q_norm[0][None, :],
                   attn_k_norm[0][None, :], pos3, inv_freq)
    x = ffn2(x, 0, mixer_out=(_moba(q, k, v), attn_w_o[0]))

    x = ffn1(x, 1)
    w_in = ssm_w_in[0]
    wz = w_in[:, :SSM_D_INNER].astype(BF16)
    wx = w_in[:, SSM_D_INNER:SSM_D_INNER + SSM_CONV_DIM].astype(BF16)
    wdt = jnp.pad(w_in[:, SSM_D_INNER + SSM_CONV_DIM:], ((0, 0), (0, LANES - SSM_HEADS))).astype(BF16)
    z, xs, bm, cm, dt = _ssm_in(x, mix_norm[1][None, :], wz, wx, wdt, ssm_conv_w[0],
                                ssm_conv_b[0][None, :], _pad_lanes(ssm_dt_bias[0]))
    d_skip_x = jnp.repeat(ssm_d[0].astype(F32), SSM_HEAD_DIM)[None, :]
    y = _ssd(xs, bm, cm, dt, z, _pad_lanes(ssm_a_log[0]), d_skip_x, ssm_norm[0][None, :])
    return ffn2(x, 1, mixer_out=(y, ssm_w_out[0]))
```

```python
import functools
import math

import numpy as np
import jax
import jax.numpy as jnp
from jax import lax
from jax.experimental import pallas as pl
from jax.experimental.pallas import tpu as pltpu

F32 = jnp.float32
BF16 = jnp.bfloat16

D_MODEL = 1024
D_FF = 2816
EPS = 1e-6

ATTN_HEADS = 8
ATTN_HEAD_DIM = 128
MOBA_BLOCK = 256
MOBA_TOPK = 3
ROPE_THETA = 10000.0

SSM_D_INNER = 2048
SSM_HEAD_DIM = 64
SSM_HEADS = 32
SSM_GROUPS = 8
SSM_STATE = 128
SSM_CONV = 4
SSM_CHUNK = 128
SSM_BC_DIM = SSM_GROUPS * SSM_STATE
SSM_CONV_DIM = SSM_D_INNER + 2 * SSM_BC_DIM

LANES = 128
VMEM_LIMIT_BYTES = 56 * 1024 * 1024

FFN_ROWS = 512
FFN_COLS = 256
PROJ_ROWS = 512
SSM_IN_ROWS = 256
SSD_CHUNKS_PER_STEP = 4
SUBLANES = 8
HALO_ROWS = 16


def _resident(shape):
    nd = len(shape)
    return pl.BlockSpec(shape, lambda *_: (0,) * nd, pipeline_mode=pl.Buffered(1))


def _params(n_axes):
    return pltpu.CompilerParams(dimension_semantics=("arbitrary",) * n_axes,
                                vmem_limit_bytes=VMEM_LIMIT_BYTES)


def _rms(x, g):
    ms = jnp.mean(x * x, axis=-1, keepdims=True)
    return x * lax.rsqrt(ms + EPS) * g


def _silu(x):
    return x * jax.nn.sigmoid(x)


def _dot(a, b):
    return jnp.dot(a, b, preferred_element_type=F32)


def _dot_nt(a, b):
    return lax.dot_general(a, b, (((1,), (1,)), ((), ())), preferred_element_type=F32)


def _split2(v):
    hi = v.astype(BF16)
    return hi, (v - hi.astype(F32)).astype(BF16)


def _split3(v):
    hi = v.astype(BF16)
    r1 = v - hi.astype(F32)
    mid = r1.astype(BF16)
    lo = (r1 - mid.astype(F32)).astype(BF16)
    return hi, mid, lo


def _ffn_kernel(*refs, has_mixer_out):
    if has_mixer_out:
        x_ref, a_ref, wp_ref, g_ref, wg_ref, wu_ref, wd_ref, o_ref = refs
        x = x_ref[...] + _dot(a_ref[...], wp_ref[...])
    else:
        x_ref, g_ref, wg_ref, wu_ref, wd_ref, o_ref = refs
        x = x_ref[...]
    n = _rms(x, g_ref[...]).astype(BF16)
    acc = jnp.zeros(x.shape, F32)
    for c in range(D_FF // FFN_COLS):
        sl = slice(c * FFN_COLS, (c + 1) * FFN_COLS)
        gate = _dot(n, wg_ref[:, sl])
        up = _dot(n, wu_ref[:, sl])
        h = (_silu(gate) * up).astype(BF16)
        acc = acc + _dot(h, wd_ref[sl, :])
    o_ref[...] = x + 0.5 * acc


def _ffn(x2d, g, wg, wu, wd, mixer_out=None):
    m = x2d.shape[0]
    row = pl.BlockSpec((FFN_ROWS, D_MODEL), lambda i: (i, 0))
    weights = [_resident((1, D_MODEL)), _resident((D_MODEL, D_FF)),
               _resident((D_MODEL, D_FF)), _resident((D_FF, D_MODEL))]
    if mixer_out is None:
        operands, specs = (x2d, g, wg, wu, wd), [row] + weights
    else:
        a2d, wp = mixer_out
        k = a2d.shape[1]
        operands = (x2d, a2d, wp, g, wg, wu, wd)
        specs = [row, pl.BlockSpec((FFN_ROWS, k), lambda i: (i, 0)),
                 _resident((k, D_MODEL))] + weights
    return pl.pallas_call(
        functools.partial(_ffn_kernel, has_mixer_out=mixer_out is not None),
        grid=(m // FFN_ROWS,),
        in_specs=specs,
        out_specs=row,
        out_shape=jax.ShapeDtypeStruct(x2d.shape, F32),
        compiler_params=_params(1),
        name="ffn",
    )(*operands)


def _qkv_kernel(x_ref, g_ref, w_ref, qg_ref, kg_ref, pos_ref, invf_ref, hmean_ref,
                q_ref, k_ref, v_ref):
    n = _rms(x_ref[0], g_ref[...]).astype(BF16)
    half_dim = ATTN_HEAD_DIM // 2
    half_rows = pos_ref.shape[1] // 2
    pos = pos_ref[0].astype(F32)
    lane = lax.broadcasted_iota(jnp.int32, (half_rows, ATTN_HEAD_DIM), 1)
    low = lane < half_dim
    ang = jnp.where(low, pos[:half_rows], pos[half_rows:]) * invf_ref[...]

    def spread(t):
        swapped = pltpu.roll(t, half_dim, axis=1)
        return jnp.concatenate([jnp.where(low, t, swapped), jnp.where(low, swapped, t)], axis=0)

    cos = spread(jnp.cos(ang))
    sin = spread(jnp.sin(ang))
    lane = lax.broadcasted_iota(jnp.int32, cos.shape, 1)
    sin_signed = jnp.where(lane < half_dim, -sin, sin)

    def gain_tables(gain):
        return gain * cos, pltpu.roll(gain, half_dim, axis=1) * sin_signed

    def norm_rope_pair(t2, tables, out_ref, h0):
        ms = _dot(jnp.concatenate(_split2(t2 * t2), axis=1), hmean_ref[...])
        u2 = t2 * lax.rsqrt(ms + EPS)
        for i in range(2):
            u = u2[:, i * ATTN_HEAD_DIM:(i + 1) * ATTN_HEAD_DIM]
            out_ref[0, h0 + i] = (u * tables[0]
                                  + pltpu.roll(u, half_dim, axis=1) * tables[1]).astype(BF16)

    q_tables = gain_tables(qg_ref[...] * (ATTN_HEAD_DIM ** -0.5 * math.log2(math.e)))
    k_tables = gain_tables(kg_ref[...])
    pair = 2 * ATTN_HEAD_DIM
    pieces = [(part, h0) for h0 in range(0, ATTN_HEADS, 2) for part in range(3)]
    project = lambda part, h0: _dot(n, w_ref[:, part * D_MODEL + h0 * ATTN_HEAD_DIM:
                                             part * D_MODEL + h0 * ATTN_HEAD_DIM + pair])
    ahead = project(*pieces[0])
    for i, (part, h0) in enumerate(pieces):
        t2 = ahead
        if i + 1 < len(pieces):
            ahead = project(*pieces[i + 1])
        if part == 0:
            norm_rope_pair(t2, q_tables, q_ref, h0)
        elif part == 1:
            norm_rope_pair(t2, k_tables, k_ref, h0)
        else:
            for i_head in range(2):
                v_ref[0, h0 + i_head] = t2[:, i_head * ATTN_HEAD_DIM:
                                           (i_head + 1) * ATTN_HEAD_DIM].astype(BF16)


def _qkv(x, g, w, q_gain, k_gain, pos3, inv_freq):
    b, t, _ = x.shape
    pair = 2 * ATTN_HEAD_DIM
    same_head = (np.arange(2 * pair)[:, None] % pair) // ATTN_HEAD_DIM == np.arange(pair)[None, :] // ATTN_HEAD_DIM
    head_mean = jnp.asarray(same_head.astype(np.float32) / ATTN_HEAD_DIM, BF16)
    head_out = pl.BlockSpec((1, ATTN_HEADS, PROJ_ROWS, ATTN_HEAD_DIM), lambda i, j: (i, 0, j, 0))
    shape = jax.ShapeDtypeStruct((b, ATTN_HEADS, t, ATTN_HEAD_DIM), BF16)
    return pl.pallas_call(
        _qkv_kernel,
        grid=(b, t // PROJ_ROWS),
        in_specs=[pl.BlockSpec((1, PROJ_ROWS, D_MODEL), lambda i, j: (i, j, 0)),
                  _resident((1, D_MODEL)), _resident((D_MODEL, 3 * D_MODEL)),
                  _resident((1, ATTN_HEAD_DIM)), _resident((1, ATTN_HEAD_DIM)),
                  pl.BlockSpec((1, PROJ_ROWS, 1), lambda i, j: (i, j, 0)),
                  _resident((1, ATTN_HEAD_DIM)), _resident(head_mean.shape)],
        out_specs=[head_out, head_out, head_out],
        out_shape=[shape, shape, shape],
        compiler_params=_params(2),
        name="qkv_rope",
    )(x, g, w, q_gain, k_gain, pos3, inv_freq, head_mean)


def _moba_kernel(q_ref, k_ref, v_ref, o_ref, vt_ref, s_ref, p_ref):
    blk = MOBA_BLOCK
    t = q_ref.shape[2]
    n_blk = t // blk
    q_all = q_ref[0, 0]
    vt_ref[...] = v_ref[0, 0].T

    means = [jnp.mean(k_ref[0, 0, n * blk:(n + 1) * blk, :].astype(F32), axis=0, keepdims=True)
             for n in range(n_blk)]
    gate = _dot_nt(jnp.concatenate(means, axis=0).astype(BF16), q_all)
    key_blk = lax.broadcasted_iota(jnp.int32, gate.shape, 0)
    qry_blk = lax.broadcasted_iota(jnp.int32, gate.shape, 1) // blk
    past = key_blk < qry_blk
    gate = jnp.where(past, gate, -jnp.inf)
    rank = jnp.zeros(gate.shape, F32)
    for n in range(n_blk):
        other = gate[n:n + 1, :]
        rank = (rank + jnp.where(other > gate, 1.0, 0.0)
                + jnp.where(other == gate, jnp.where(key_blk > n, 1.0, 0.0), 0.0))
    bias = jnp.where(past & (rank < MOBA_TOPK), 0.0, -jnp.inf)

    causal = (lax.broadcasted_iota(jnp.int32, (blk, blk), 0)
              <= lax.broadcasted_iota(jnp.int32, (blk, blk), 1))

    rows = lambda n: slice(n * blk, (n + 1) * blk)

    def raw_scores(qi, n):
        s_ref[qi % 2, rows(n), :] = _dot_nt(k_ref[0, 0, rows(n), :], q_ref[0, 0, rows(qi), :])

    raw_scores(0, 0)
    for qi in range(n_blk):
        pending = list(range(qi + 2)) if qi + 1 < n_blk else []
        slot = qi % 2
        q_lo = qi * blk
        n_keys = q_lo + blk
        diag = jnp.where(causal, s_ref[slot, rows(qi), :], -jnp.inf)
        maxes = [jnp.max(s_ref[slot, rows(n), :], axis=0, keepdims=True)
                 + bias[n:n + 1, q_lo:n_keys] for n in range(qi)]
        maxes.append(jnp.max(diag, axis=0, keepdims=True))
        m = functools.reduce(jnp.maximum, maxes)
        l = jnp.zeros((1, blk), F32)
        for n in range(qi + 1):
            if n < qi:
                p = jnp.exp2(s_ref[slot, rows(n), :] + (bias[n:n + 1, q_lo:n_keys] - m))
            else:
                p = jnp.exp2(diag - m)
            l = l + jnp.sum(p, axis=0, keepdims=True)
            p_ref[slot, rows(n), :] = p.astype(BF16)
            if pending:
                raw_scores(qi + 1, pending.pop(0))
        for n in pending:
            raw_scores(qi + 1, n)
        o_t = _dot(vt_ref[:, :n_keys], p_ref[slot, :n_keys, :]) / l
        o_ref[0, q_lo:n_keys, :] = o_t.astype(BF16).T


def _moba(q, k, v):
    b, h, t, d = q.shape
    head = pl.BlockSpec((1, 1, t, d), lambda i, j: (i, j, 0, 0))
    return pl.pallas_call(
        _moba_kernel,
        grid=(b, h),
        in_specs=[head, head, head],
        out_specs=pl.BlockSpec((1, t, d), lambda i, j: (i, 0, j)),
        out_shape=jax.ShapeDtypeStruct((b, t, h * d), BF16),
        scratch_shapes=[pltpu.VMEM((d, t), BF16), pltpu.VMEM((2, t, MOBA_BLOCK), F32),
                        pltpu.VMEM((2, t, MOBA_BLOCK), BF16)],
        compiler_params=_params(2),
        name="moba_attn",
    )(q, k, v)


def _ssm_in_kernel(x_ref, halo_ref, g_ref, wx_ref, wdt_ref, cw_ref, cb_ref, dtb_ref,
                   hn_ref, xs_ref, b_ref, c_ref, dt_ref):
    j = pl.program_id(1)
    rows = x_ref.shape[1]
    g = g_ref[...]
    n = _rms(x_ref[0], g).astype(BF16)
    n_halo = _rms(halo_ref[0], g).astype(BF16)
    halo_keep = jnp.where(j > 0, 1.0, 0.0)
    sub = lax.broadcasted_iota(jnp.int32, (rows // SUBLANES, SUBLANES, SSM_CONV_DIM), 1)

    def shift_rows(cur3, prev8, back):
        rot = pltpu.roll(cur3, back, axis=1)
        rot_prev = jnp.concatenate([pltpu.roll(prev8, back, axis=0)[None], rot[:-1]], axis=0)
        return jnp.where(sub >= back, rot, rot_prev)

    hn_ref[0] = n
    prev8 = _dot(n_halo, wx_ref[...])[HALO_ROWS - SUBLANES:] * halo_keep
    cur3 = _dot(n, wx_ref[...]).reshape(sub.shape)
    w = [cw_ref[k:k + 1, :] for k in range(SSM_CONV)]
    back1 = shift_rows(cur3, prev8, 1)
    near = cur3 * w[3] + back1 * w[2]
    far = cur3 * w[1] + back1 * w[0]
    far_prev8 = prev8 * w[1] + pltpu.roll(prev8, 1, axis=0) * w[0]
    y = _silu(cb_ref[...] + near + shift_rows(far, far_prev8, 2)).reshape(rows, SSM_CONV_DIM)
    xs_ref[0] = y[:, :SSM_D_INNER]
    b_ref[0] = y[:, SSM_D_INNER:SSM_D_INNER + SSM_BC_DIM].astype(BF16)
    c_ref[0] = y[:, SSM_D_INNER + SSM_BC_DIM:].astype(BF16)

    dt_raw = _dot(n, wdt_ref[...]) + dtb_ref[...]
    softplus = jnp.maximum(dt_raw, 0.0) + jnp.log1p(jnp.exp(-jnp.abs(dt_raw)))
    lane = lax.broadcasted_iota(jnp.int32, dt_raw.shape, 1)
    dt_ref[0] = jnp.where(lane < SSM_HEADS, softplus, 0.0)


def _ssm_in(x, g, wx, wdt, conv_w, conv_b, dt_bias):
    b, t, _ = x.shape
    rows = SSM_IN_ROWS
    halo_blocks = rows // HALO_ROWS

    def tok(width, dtype):
        return (pl.BlockSpec((1, rows, width), lambda i, j: (i, j, 0)),
                jax.ShapeDtypeStruct((b, t, width), dtype))

    outs = [tok(D_MODEL, BF16), tok(SSM_D_INNER, F32), tok(SSM_BC_DIM, BF16),
            tok(SSM_BC_DIM, BF16), tok(LANES, F32)]
    return pl.pallas_call(
        _ssm_in_kernel,
        grid=(b, t // rows),
        in_specs=[pl.BlockSpec((1, rows, D_MODEL), lambda i, j: (i, j, 0)),
                  pl.BlockSpec((1, HALO_ROWS, D_MODEL),
                               lambda i, j: (i, jnp.maximum(j * halo_blocks - 1, 0), 0)),
                  _resident((1, D_MODEL)),
                  _resident((D_MODEL, SSM_CONV_DIM)), _resident((D_MODEL, LANES)),
                  _resident((SSM_CONV, SSM_CONV_DIM)), _resident((1, SSM_CONV_DIM)),
                  _resident((1, LANES))],
        out_specs=[o[0] for o in outs],
        out_shape=[o[1] for o in outs],
        compiler_params=_params(2),
        name="ssm_in",
    )(x, x, g, wx, wdt, conv_w, conv_b, dt_bias)


def _ssd_kernel(xs_ref, b_ref, c_ref, dt_ref, hn_ref, wz_ref, alog_ref, dskip_ref, ng_ref,
                tri_ref, ehead_ref, y_ref, state_ref, ybuf_ref, xdt_ref, xdd_ref):
    @pl.when(pl.program_id(1) == 0)
    def _():
        state_ref[...] = jnp.zeros(state_ref.shape, F32)

    for k in range(SSD_CHUNKS_PER_STEP):
        rows = slice(k * SSM_CHUNK, (k + 1) * SSM_CHUNK)
        _ssd_chunk(xs_ref.at[0, rows], b_ref.at[0, rows], c_ref.at[0, rows], dt_ref.at[0, rows],
                   hn_ref.at[0, rows], wz_ref, alog_ref, dskip_ref, ng_ref, tri_ref, ehead_ref,
                   y_ref.at[0, rows], state_ref, ybuf_ref.at[k], xdt_ref.at[k], xdd_ref.at[k])


def _ssd_chunk(xs_ref, b_ref, c_ref, dt_ref, hn_ref, wz_ref, alog_ref, dskip_ref, ng_ref, tri_ref,
               ehead_ref, y_ref, state_ref, ybuf_ref, xdt_ref, xdd_ref):
    ch = SSM_CHUNK
    heads_per_group = SSM_HEADS // SSM_GROUPS
    group_w = heads_per_group * SSM_HEAD_DIM
    hn = hn_ref[...]
    z_parts = []
    dt = dt_ref[...]
    a = dt * (-jnp.exp(alog_ref[...]) * math.log2(math.e))
    a_cs = _dot(tri_ref[...], jnp.concatenate(_split3(a), axis=0))
    a_last = a_cs[ch - 1:ch, :]
    decay = jnp.exp2(a_last - a_cs)
    stack = jnp.concatenate([dt, decay], axis=0)
    wide = _dot(jnp.concatenate(_split2(stack), axis=1), ehead_ref[:2 * LANES, :])
    dt_x = wide[:ch]
    decay_x = wide[ch:]
    tail = jnp.concatenate([jnp.exp2(a_last), jnp.zeros((SUBLANES - 1, LANES), F32)], axis=0)
    chunk_decay_x = _dot(jnp.concatenate(_split3(tail), axis=1), ehead_ref[...])[:1]

    acs_row = a_cs.T

    x = xs_ref[...]
    xdt = x * dt_x
    xdt_b = xdt.astype(BF16)
    xdd_b = (xdt * decay_x).astype(BF16)
    for p in range(SSM_HEADS // 2):
        xdt_ref[p] = xdt_b[:, p * LANES:(p + 1) * LANES]
    for g in range(SSM_GROUPS):
        xdd_ref[g] = xdd_b[:, g * group_w:(g + 1) * group_w]
    tril =(lax.broadcasted_iota(jnp.int32, (ch, ch), 0)
            >= lax.broadcasted_iota(jnp.int32, (ch, ch), 1))
    lane = lax.broadcasted_iota(jnp.int32, (ch, LANES), 1)

    for g in range(SSM_GROUPS):
        bg = b_ref[:, g * SSM_STATE:(g + 1) * SSM_STATE]
        cg = c_ref[:, g * SSM_STATE:(g + 1) * SSM_STATE]
        cb = _dot_nt(cg, bg)
        bg_t = bg.T
        gl = g * group_w
        state_g = state_ref[:, gl:gl + group_w]
        lhs = []
        for r in range(heads_per_group):
            h = g * heads_per_group + r
            row = jnp.broadcast_to(acs_row[h:h + 1, :], (ch, LANES))
            col = row.T
            seg = col - row
            decay_ls = jnp.exp2(jnp.where(tril, seg, -jnp.inf))
            lhs.append(jnp.concatenate([(cb * decay_ls).astype(BF16),
                                        (cg.astype(F32) * jnp.exp2(col)).astype(BF16)], axis=1))
        for p in range(heads_per_group // 2):
            pair = g * (heads_per_group // 2) + p
            rhs = jnp.concatenate([xdt_ref[pair],
                                   state_g[:, p * LANES:(p + 1) * LANES].astype(BF16)], axis=0)
            y0 = _dot(lhs[2 * p], rhs)
            y1 = _dot(lhs[2 * p + 1], rhs)
            ybuf_ref[pair] = jnp.where(lane < SSM_HEAD_DIM, y0, y1)
        new_state = _dot(bg_t, xdd_ref[g])
        state_ref[:, gl:gl + group_w] = state_g * chunk_decay_x[:, gl:gl + group_w] + new_state
        z_parts.append(_dot(hn, wz_ref[:, gl:gl + group_w]))

    y_ssd = jnp.concatenate([ybuf_ref[p] for p in range(SSM_HEADS // 2)], axis=1)
    z = jnp.concatenate(z_parts, axis=1)
    y = (y_ssd + dskip_ref[...] * x) * _silu(z)
    for g in range(SSM_GROUPS):
        gl = g * group_w
        yg = y[:, gl:gl + group_w]
        ms = jnp.mean(yg * yg, axis=-1, keepdims=True)
        y_ref[:, gl:gl + group_w] = (yg * lax.rsqrt(ms + EPS)
                                     * ng_ref[:, gl:gl + group_w]).astype(BF16)


def _ssd_constants():
    ch = SSM_CHUNK
    tri = np.tril(np.ones((ch, ch), np.float32))
    ehead = np.zeros((LANES, SSM_D_INNER), np.float32)
    for h in range(SSM_HEADS):
        ehead[h, h * SSM_HEAD_DIM:(h + 1) * SSM_HEAD_DIM] = 1.0
    as_bf16 = lambda m: jnp.asarray(m, BF16)
    return as_bf16(np.tile(tri, (1, 3))), as_bf16(np.tile(ehead, (3, 1)))


def _ssd(xs, bm, cm, dt, hn, wz, a_log, d_skip_x, norm_g):
    b, t, _ = xs.shape
    rows = SSD_CHUNKS_PER_STEP * SSM_CHUNK
    tri3, ehead3 = _ssd_constants()

    def tok(width):
        return pl.BlockSpec((1, rows, width), lambda i, j: (i, j, 0))

    return pl.pallas_call(
        _ssd_kernel,
        grid=(b, t // rows),
        in_specs=[tok(SSM_D_INNER), tok(SSM_BC_DIM), tok(SSM_BC_DIM), tok(LANES), tok(D_MODEL),
                  _resident((D_MODEL, SSM_D_INNER)), _resident((1, LANES)), _resident((1, SSM_D_INNER)), _resident((1, SSM_D_INNER)),
                  _resident(tri3.shape), _resident(ehead3.shape)],
        out_specs=tok(SSM_D_INNER),
        out_shape=jax.ShapeDtypeStruct((b, t, SSM_D_INNER), BF16),
        scratch_shapes=[pltpu.VMEM((SSM_STATE, SSM_D_INNER), F32),
                        pltpu.VMEM((SSD_CHUNKS_PER_STEP, SSM_HEADS // 2, SSM_CHUNK, LANES), F32),
                        pltpu.VMEM((SSD_CHUNKS_PER_STEP, SSM_HEADS // 2, SSM_CHUNK, LANES), BF16),
                        pltpu.VMEM((SSD_CHUNKS_PER_STEP, SSM_GROUPS, SSM_CHUNK, 2 * LANES), BF16)],
        compiler_params=_params(2),
        name="ssd",
    )(xs, bm, cm, dt, hn, wz, a_log, d_skip_x, norm_g, tri3, ehead3)


def _pad_lanes(v):
    return jnp.pad(v.astype(F32), (0, LANES - v.shape[0]))[None, :]


def kernel(x, positions, ffn1_norm, ffn1_w_gate, ffn1_w_up, ffn1_w_down, mix_norm, ffn2_norm, ffn2_w_gate, ffn2_w_up, ffn2_w_down, attn_w_qkv, attn_q_norm, attn_k_norm, attn_w_o, ssm_w_in, ssm_conv_w, ssm_conv_b, ssm_dt_bias, ssm_a_log, ssm_d, ssm_norm, ssm_w_out):
    bsz, seq, d = x.shape
    rows = bsz * seq
    half = ATTN_HEAD_DIM // 2
    inv_freq = ROPE_THETA ** (-jnp.arange(half, dtype=F32) / half)
    inv_freq = jnp.concatenate([inv_freq, inv_freq])[None, :]
    pos3 = positions[:, :, None]

    def ffn(xc, layer, norm, wg, wu, wd, mixer_out=None):
        if mixer_out is not None:
            mixer_out = (mixer_out[0].reshape(rows, -1), mixer_out[1].astype(BF16))
        out = _ffn(xc.reshape(rows, d), norm[layer][None, :], wg[layer].astype(BF16),
                   wu[layer].astype(BF16), wd[layer].astype(BF16), mixer_out)
        return out.reshape(bsz, seq, d)

    ffn1 = functools.partial(ffn, norm=ffn1_norm, wg=ffn1_w_gate, wu=ffn1_w_up, wd=ffn1_w_down)
    ffn2 = functools.partial(ffn, norm=ffn2_norm, wg=ffn2_w_gate, wu=ffn2_w_up, wd=ffn2_w_down)

    x = ffn1(x, 0)
    q, k, v = _qkv(x, mix_norm[0][None, :], attn_w_qkv[0].astype(BF16), attn_q_norm[0][None, :],
                   attn_k_norm[0][None, :], pos3, inv_freq)
    x = ffn2(x, 0, mixer_out=(_moba(q, k, v), attn_w_o[0]))

    x = ffn1(x, 1)
    w_in = ssm_w_in[0]
    wz = w_in[:, :SSM_D_INNER].astype(BF16)
    wx = w_in[:, SSM_D_INNER:SSM_D_INNER + SSM_CONV_DIM].astype(BF16)
    wdt = jnp.pad(w_in[:, SSM_D_INNER + SSM_CONV_DIM:], ((0, 0), (0, LANES - SSM_HEADS))).astype(BF16)
    hn, xs, bm, cm, dt = _ssm_in(x, mix_norm[1][None, :], wx, wdt, ssm_conv_w[0],
                                 ssm_conv_b[0][None, :], _pad_lanes(ssm_dt_bias[0]))
    d_skip_x = jnp.repeat(ssm_d[0].astype(F32), SSM_HEAD_DIM)[None, :]
    y = _ssd(xs, bm, cm, dt, hn, wz, _pad_lanes(ssm_a_log[0]), d_skip_x, ssm_norm[0][None, :])
    return ffn2(x, 1, mixer_out=(y, ssm_w_out[0]))
```

```python
import functools
import math

import numpy as np
import jax
import jax.numpy as jnp
from jax import lax
from jax.experimental import pallas as pl
from jax.experimental.pallas import tpu as pltpu

F32 = jnp.float32
BF16 = jnp.bfloat16

D_MODEL = 1024
D_FF = 2816
EPS = 1e-6

ATTN_HEADS = 8
ATTN_HEAD_DIM = 128
MOBA_BLOCK = 256
MOBA_TOPK = 3
ROPE_THETA = 10000.0

SSM_D_INNER = 2048
SSM_HEAD_DIM = 64
SSM_HEADS = 32
SSM_GROUPS = 8
SSM_STATE = 128
SSM_CONV = 4
SSM_CHUNK = 128
SSM_BC_DIM = SSM_GROUPS * SSM_STATE
SSM_CONV_DIM = SSM_D_INNER + 2 * SSM_BC_DIM

LANES = 128
VMEM_LIMIT_BYTES = 56 * 1024 * 1024

FFN_ROWS = 512
FFN_COLS = 256
PROJ_ROWS = 512
SSM_IN_ROWS = 256
SSD_CHUNKS_PER_STEP = 4
SUBLANES = 8
HALO_ROWS = 16


def _resident(shape):
    nd = len(shape)
    return pl.BlockSpec(shape, lambda *_: (0,) * nd, pipeline_mode=pl.Buffered(1))


def _params(n_axes):
    return pltpu.CompilerParams(dimension_semantics=("arbitrary",) * n_axes,
                                vmem_limit_bytes=VMEM_LIMIT_BYTES)


def _rms(x, g):
    ms = jnp.mean(x * x, axis=-1, keepdims=True)
    return x * lax.rsqrt(ms + EPS) * g


def _silu(x):
    return x * jax.nn.sigmoid(x)


def _dot(a, b):
    return jnp.dot(a, b, preferred_element_type=F32)


def _dot_nt(a, b):
    return lax.dot_general(a, b, (((1,), (1,)), ((), ())), preferred_element_type=F32)


def _split2(v):
    hi = v.astype(BF16)
    return hi, (v - hi.astype(F32)).astype(BF16)


def _split3(v):
    hi = v.astype(BF16)
    r1 = v - hi.astype(F32)
    mid = r1.astype(BF16)
    lo = (r1 - mid.astype(F32)).astype(BF16)
    return hi, mid, lo


def _ffn_kernel(*refs, has_mixer_out):
    if has_mixer_out:
        x_ref, a_ref, wp_ref, g_ref, wg_ref, wu_ref, wd_ref, o_ref = refs
        x = x_ref[...] + _dot(a_ref[...], wp_ref[...])
    else:
        x_ref, g_ref, wg_ref, wu_ref, wd_ref, o_ref = refs
        x = x_ref[...]
    n = _rms(x, g_ref[...]).astype(BF16)
    acc = jnp.zeros(x.shape, F32)
    for c in range(D_FF // FFN_COLS):
        sl = slice(c * FFN_COLS, (c + 1) * FFN_COLS)
        gate = _dot(n, wg_ref[:, sl])
        up = _dot(n, wu_ref[:, sl])
        h = (_silu(gate) * up).astype(BF16)
        acc = acc + _dot(h, wd_ref[sl, :])
    o_ref[...] = x + 0.5 * acc


def _ffn(x2d, g, wg, wu, wd, mixer_out=None):
    m = x2d.shape[0]
    row = pl.BlockSpec((FFN_ROWS, D_MODEL), lambda i: (i, 0))
    weights = [_resident((1, D_MODEL)), _resident((D_MODEL, D_FF)),
               _resident((D_MODEL, D_FF)), _resident((D_FF, D_MODEL))]
    if mixer_out is None:
        operands, specs = (x2d, g, wg, wu, wd), [row] + weights
    else:
        a2d, wp = mixer_out
        k = a2d.shape[1]
        operands = (x2d, a2d, wp, g, wg, wu, wd)
        specs = [row, pl.BlockSpec((FFN_ROWS, k), lambda i: (i, 0)),
                 _resident((k, D_MODEL))] + weights
    return pl.pallas_call(
        functools.partial(_ffn_kernel, has_mixer_out=mixer_out is not None),
        grid=(m // FFN_ROWS,),
        in_specs=specs,
        out_specs=row,
        out_shape=jax.ShapeDtypeStruct(x2d.shape, F32),
        compiler_params=_params(1),
        name="ffn",
    )(*operands)


def _qkv_kernel(x_ref, g_ref, w_ref, qg_ref, kg_ref, pos_ref, invf_ref, hmean_ref,
                q_ref, k_ref, v_ref):
    n = _rms(x_ref[0], g_ref[...]).astype(BF16)
    half_dim = ATTN_HEAD_DIM // 2
    half_rows = pos_ref.shape[1] // 2
    pos = pos_ref[0].astype(F32)
    lane = lax.broadcasted_iota(jnp.int32, (half_rows, ATTN_HEAD_DIM), 1)
    low = lane < half_dim
    ang = jnp.where(low, pos[:half_rows], pos[half_rows:]) * invf_ref[...]

    def spread(t):
        swapped = pltpu.roll(t, half_dim, axis=1)
        return jnp.concatenate([jnp.where(low, t, swapped), jnp.where(low, swapped, t)], axis=0)

    cos = spread(jnp.cos(ang))
    sin = spread(jnp.sin(ang))
    lane = lax.broadcasted_iota(jnp.int32, cos.shape, 1)
    sin_signed = jnp.where(lane < half_dim, -sin, sin)

    def gain_tables(gain):
        return gain * cos, pltpu.roll(gain, half_dim, axis=1) * sin_signed

    def norm_rope_pair(t2, tables, out_ref, h0):
        ms = _dot(jnp.concatenate(_split2(t2 * t2), axis=1), hmean_ref[...])
        u2 = t2 * lax.rsqrt(ms + EPS)
        for i in range(2):
            u = u2[:, i * ATTN_HEAD_DIM:(i + 1) * ATTN_HEAD_DIM]
            out_ref[0, h0 + i] = (u * tables[0]
                                  + pltpu.roll(u, half_dim, axis=1) * tables[1]).astype(BF16)

    q_tables = gain_tables(qg_ref[...] * (ATTN_HEAD_DIM ** -0.5 * math.log2(math.e)))
    k_tables = gain_tables(kg_ref[...])
    pair = 2 * ATTN_HEAD_DIM
    pieces = [(part, h0) for h0 in range(0, ATTN_HEADS, 2) for part in range(3)]
    project = lambda part, h0: _dot(n, w_ref[:, part * D_MODEL + h0 * ATTN_HEAD_DIM:
                                             part * D_MODEL + h0 * ATTN_HEAD_DIM + pair])
    ahead = project(*pieces[0])
    for i, (part, h0) in enumerate(pieces):
        t2 = ahead
        if i + 1 < len(pieces):
            ahead = project(*pieces[i + 1])
        if part == 0:
            norm_rope_pair(t2, q_tables, q_ref, h0)
        elif part == 1:
            norm_rope_pair(t2, k_tables, k_ref, h0)
        else:
            for i_head in range(2):
                v_ref[0, h0 + i_head] = t2[:, i_head * ATTN_HEAD_DIM:
                                           (i_head + 1) * ATTN_HEAD_DIM].astype(BF16)


def _qkv(x, g, w, q_gain, k_gain, pos3, inv_freq):
    b, t, _ = x.shape
    pair = 2 * ATTN_HEAD_DIM
    same_head = (np.arange(2 * pair)[:, None] % pair) // ATTN_HEAD_DIM == np.arange(pair)[None, :] // ATTN_HEAD_DIM
    head_mean = jnp.asarray(same_head.astype(np.float32) / ATTN_HEAD_DIM, BF16)
    head_out = pl.BlockSpec((1, ATTN_HEADS, PROJ_ROWS, ATTN_HEAD_DIM), lambda i, j: (i, 0, j, 0))
    shape = jax.ShapeDtypeStruct((b, ATTN_HEADS, t, ATTN_HEAD_DIM), BF16)
    return pl.pallas_call(
        _qkv_kernel,
        grid=(b, t // PROJ_ROWS),
        in_specs=[pl.BlockSpec((1, PROJ_ROWS, D_MODEL), lambda i, j: (i, j, 0)),
                  _resident((1, D_MODEL)), _resident((D_MODEL, 3 * D_MODEL)),
                  _resident((1, ATTN_HEAD_DIM)), _resident((1, ATTN_HEAD_DIM)),
                  pl.BlockSpec((1, PROJ_ROWS, 1), lambda i, j: (i, j, 0)),
                  _resident((1, ATTN_HEAD_DIM)), _resident(head_mean.shape)],
        out_specs=[head_out, head_out, head_out],
        out_shape=[shape, shape, shape],
        compiler_params=_params(2),
        name="qkv_rope",
    )(x, g, w, q_gain, k_gain, pos3, inv_freq, head_mean)


def _moba_kernel(q_ref, k_ref, v_ref, o_ref, vt_ref, s_ref, p_ref):
    blk = MOBA_BLOCK
    t = q_ref.shape[2]
    n_blk = t // blk
    q_all = q_ref[0, 0]
    vt_ref[...] = v_ref[0, 0].T

    means = [jnp.mean(k_ref[0, 0, n * blk:(n + 1) * blk, :].astype(F32), axis=0, keepdims=True)
             for n in range(n_blk)]
    gate = _dot_nt(jnp.concatenate(means, axis=0).astype(BF16), q_all)
    key_blk = lax.broadcasted_iota(jnp.int32, gate.shape, 0)
    qry_blk = lax.broadcasted_iota(jnp.int32, gate.shape, 1) // blk
    past = key_blk < qry_blk
    gate = jnp.where(past, gate, -jnp.inf)
    rank = jnp.zeros(gate.shape, F32)
    for n in range(n_blk):
        other = gate[n:n + 1, :]
        rank = (rank + jnp.where(other > gate, 1.0, 0.0)
                + jnp.where(other == gate, jnp.where(key_blk > n, 1.0, 0.0), 0.0))
    bias = jnp.where(past & (rank < MOBA_TOPK), 0.0, -jnp.inf)

    causal = (lax.broadcasted_iota(jnp.int32, (blk, blk), 0)
              <= lax.broadcasted_iota(jnp.int32, (blk, blk), 1))

    rows = lambda n: slice(n * blk, (n + 1) * blk)

    def raw_scores(qi, n):
        s_ref[qi % 2, rows(n), :] = _dot_nt(k_ref[0, 0, rows(n), :], q_ref[0, 0, rows(qi), :])

    raw_scores(0, 0)
    for qi in range(n_blk):
        pending = list(range(qi + 2)) if qi + 1 < n_blk else []
        slot = qi % 2
        q_lo = qi * blk
        n_keys = q_lo + blk
        diag = jnp.where(causal, s_ref[slot, rows(qi), :], -jnp.inf)
        maxes = [jnp.max(s_ref[slot, rows(n), :], axis=0, keepdims=True)
                 + bias[n:n + 1, q_lo:n_keys] for n in range(qi)]
        maxes.append(jnp.max(diag, axis=0, keepdims=True))
        m = functools.reduce(jnp.maximum, maxes)
        l = jnp.zeros((1, blk), F32)
        for n in range(qi + 1):
            if n < qi:
                p = jnp.exp2(s_ref[slot, rows(n), :] + (bias[n:n + 1, q_lo:n_keys] - m))
            else:
                p = jnp.exp2(diag - m)
            l = l + jnp.sum(p, axis=0, keepdims=True)
            p_ref[slot, rows(n), :] = p.astype(BF16)
            if pending:
                raw_scores(qi + 1, pending.pop(0))
        for n in pending:
            raw_scores(qi + 1, n)
        o_t = _dot(vt_ref[:, :n_keys], p_ref[slot, :n_keys, :]) / l
        o_ref[0, q_lo:n_keys, :] = o_t.astype(BF16).T


def _moba(q, k, v):
    b, h, t, d = q.shape
    head = pl.BlockSpec((1, 1, t, d), lambda i, j: (i, j, 0, 0))
    return pl.pallas_call(
        _moba_kernel,
        grid=(b, h),
        in_specs=[head, head, head],
        out_specs=pl.BlockSpec((1, t, d), lambda i, j: (i, 0, j)),
        out_shape=jax.ShapeDtypeStruct((b, t, h * d), BF16),
        scratch_shapes=[pltpu.VMEM((d, t), BF16), pltpu.VMEM((2, t, MOBA_BLOCK), F32),
                        pltpu.VMEM((2, t, MOBA_BLOCK), BF16)],
        compiler_params=_params(2),
        name="moba_attn",
    )(q, k, v)


def _ssm_in_kernel(x_ref, halo_ref, g_ref, wx_ref, wdt_ref, cw_ref, cb_ref, dtb_ref,
                   hn_ref, xs_ref, b_ref, c_ref, dt_ref):
    j = pl.program_id(1)
    rows = x_ref.shape[1]
    g = g_ref[...]
    n = _rms(x_ref[0], g).astype(BF16)
    n_halo = _rms(halo_ref[0], g).astype(BF16)
    halo_keep = jnp.where(j > 0, 1.0, 0.0)
    sub = lax.broadcasted_iota(jnp.int32, (rows // SUBLANES, SUBLANES, SSM_CONV_DIM), 1)

    def shift_rows(cur3, prev8, back):
        rot = pltpu.roll(cur3, back, axis=1)
        rot_prev = jnp.concatenate([pltpu.roll(prev8, back, axis=0)[None], rot[:-1]], axis=0)
        return jnp.where(sub >= back, rot, rot_prev)

    hn_ref[0] = n
    prev8 = _dot(n_halo, wx_ref[...])[HALO_ROWS - SUBLANES:] * halo_keep
    cur3 = _dot(n, wx_ref[...]).reshape(sub.shape)
    w = [cw_ref[k:k + 1, :] for k in range(SSM_CONV)]
    back1 = shift_rows(cur3, prev8, 1)
    near = cur3 * w[3] + back1 * w[2]
    far = cur3 * w[1] + back1 * w[0]
    far_prev8 = prev8 * w[1] + pltpu.roll(prev8, 1, axis=0) * w[0]
    y = _silu(cb_ref[...] + near + shift_rows(far, far_prev8, 2)).reshape(rows, SSM_CONV_DIM)
    xs_ref[0] = y[:, :SSM_D_INNER]
    b_ref[0] = y[:, SSM_D_INNER:SSM_D_INNER + SSM_BC_DIM].astype(BF16)
    c_ref[0] = y[:, SSM_D_INNER + SSM_BC_DIM:].astype(BF16)

    dt_raw = _dot(n, wdt_ref[...]) + dtb_ref[...]
    softplus = jnp.maximum(dt_raw, 0.0) + jnp.log1p(jnp.exp(-jnp.abs(dt_raw)))
    lane = lax.broadcasted_iota(jnp.int32, dt_raw.shape, 1)
    dt_ref[0] = jnp.where(lane < SSM_HEADS, softplus, 0.0)


def _ssm_in(x, g, wx, wdt, conv_w, conv_b, dt_bias):
    b, t, _ = x.shape
    rows = SSM_IN_ROWS
    halo_blocks = rows // HALO_ROWS

    def tok(width, dtype):
        return (pl.BlockSpec((1, rows, width), lambda i, j: (i, j, 0)),
                jax.ShapeDtypeStruct((b, t, width), dtype))

    outs = [tok(D_MODEL, BF16), tok(SSM_D_INNER, F32), tok(SSM_BC_DIM, BF16),
            tok(SSM_BC_DIM, BF16), tok(LANES, F32)]
    return pl.pallas_call(
        _ssm_in_kernel,
        grid=(b, t // rows),
        in_specs=[pl.BlockSpec((1, rows, D_MODEL), lambda i, j: (i, j, 0)),
                  pl.BlockSpec((1, HALO_ROWS, D_MODEL),
                               lambda i, j: (i, jnp.maximum(j * halo_blocks - 1, 0), 0)),
                  _resident((1, D_MODEL)),
                  _resident((D_MODEL, SSM_CONV_DIM)), _resident((D_MODEL, LANES)),
                  _resident((SSM_CONV, SSM_CONV_DIM)), _resident((1, SSM_CONV_DIM)),
                  _resident((1, LANES))],
        out_specs=[o[0] for o in outs],
        out_shape=[o[1] for o in outs],
        compiler_params=_params(2),
        name="ssm_in",
    )(x, x, g, wx, wdt, conv_w, conv_b, dt_bias)


def _ssd_kernel(xs_ref, b_ref, c_ref, dt_ref, hn_ref, wz_ref, alog_ref, dskip_ref, ng_ref,
                tri_ref, ehead_ref, y_ref, state_ref, ybuf_ref, xdt_ref, xdd_ref):
    @pl.when(pl.program_id(1) == 0)
    def _():
        state_ref[...] = jnp.zeros(state_ref.shape, F32)

    for k in range(SSD_CHUNKS_PER_STEP):
        rows = slice(k * SSM_CHUNK, (k + 1) * SSM_CHUNK)
        _ssd_chunk(xs_ref.at[0, rows], b_ref.at[0, rows], c_ref.at[0, rows], dt_ref.at[0, rows],
                   hn_ref.at[0, rows], wz_ref, alog_ref, dskip_ref, ng_ref, tri_ref, ehead_ref,
                   y_ref.at[0, rows], state_ref, ybuf_ref.at[k], xdt_ref.at[k], xdd_ref.at[k])


def _ssd_chunk(xs_ref, b_ref, c_ref, dt_ref, hn_ref, wz_ref, alog_ref, dskip_ref, ng_ref, tri_ref,
               ehead_ref, y_ref, state_ref, ybuf_ref, xdt_ref, xdd_ref):
    ch = SSM_CHUNK
    heads_per_group = SSM_HEADS // SSM_GROUPS
    group_w = heads_per_group * SSM_HEAD_DIM
    hn = hn_ref[...]
    z_parts = []
    dt = dt_ref[...]
    a = dt * (-jnp.exp(alog_ref[...]) * math.log2(math.e))
    a_cs = _dot(tri_ref[...], jnp.concatenate(_split3(a), axis=0))
    cbs = [_dot_nt(c_ref[:, g * SSM_STATE:(g + 1) * SSM_STATE],
                   b_ref[:, g * SSM_STATE:(g + 1) * SSM_STATE]) for g in range(SSM_GROUPS)]
    a_last = a_cs[ch - 1:ch, :]
    decay = jnp.exp2(a_last - a_cs)
    stack = jnp.concatenate([dt, decay], axis=0)
    wide = _dot(jnp.concatenate(_split2(stack), axis=1), ehead_ref[:2 * LANES, :])
    dt_x = wide[:ch]
    decay_x = wide[ch:]
    tail = jnp.concatenate([jnp.exp2(a_last), jnp.zeros((SUBLANES - 1, LANES), F32)], axis=0)
    chunk_decay_x = _dot(jnp.concatenate(_split3(tail), axis=1), ehead_ref[...])[:1]

    acs_row = a_cs.T

    x = xs_ref[...]
    xdt = x * dt_x
    xdt_b = xdt.astype(BF16)
    xdd_b = (xdt * decay_x).astype(BF16)
    for p in range(SSM_HEADS // 2):
        xdt_ref[p] = xdt_b[:, p * LANES:(p + 1) * LANES]
    for g in range(SSM_GROUPS):
        xdd_ref[g] = xdd_b[:, g * group_w:(g + 1) * group_w]
    tril =(lax.broadcasted_iota(jnp.int32, (ch, ch), 0)
            >= lax.broadcasted_iota(jnp.int32, (ch, ch), 1))
    lane = lax.broadcasted_iota(jnp.int32, (ch, LANES), 1)

    for g in range(SSM_GROUPS):
        bg = b_ref[:, g * SSM_STATE:(g + 1) * SSM_STATE]
        cg = c_ref[:, g * SSM_STATE:(g + 1) * SSM_STATE]
        cb = cbs[g]
        bg_t = bg.T
        gl = g * group_w
        state_g = state_ref[:, gl:gl + group_w]
        lhs = []
        for r in range(heads_per_group):
            h = g * heads_per_group + r
            row = jnp.broadcast_to(acs_row[h:h + 1, :], (ch, LANES))
            col = row.T
            seg = col - row
            decay_ls = jnp.exp2(jnp.where(tril, seg, -jnp.inf))
            lhs.append(jnp.concatenate([(cb * decay_ls).astype(BF16),
                                        (cg.astype(F32) * jnp.exp2(col)).astype(BF16)], axis=1))
        for p in range(heads_per_group // 2):
            pair = g * (heads_per_group // 2) + p
            rhs = jnp.concatenate([xdt_ref[pair],
                                   state_g[:, p * LANES:(p + 1) * LANES].astype(BF16)], axis=0)
            y0 = _dot(lhs[2 * p], rhs)
            y1 = _dot(lhs[2 * p + 1], rhs)
            ybuf_ref[pair] = jnp.where(lane < SSM_HEAD_DIM, y0, y1)
        new_state = _dot(bg_t, xdd_ref[g])
        state_ref[:, gl:gl + group_w] = state_g * chunk_decay_x[:, gl:gl + group_w] + new_state
        z_parts.append(_dot(hn, wz_ref[:, gl:gl + group_w]))

    y_ssd = jnp.concatenate([ybuf_ref[p] for p in range(SSM_HEADS // 2)], axis=1)
    z = jnp.concatenate(z_parts, axis=1)
    y = (y_ssd + dskip_ref[...] * x) * _silu(z)
    for g in range(SSM_GROUPS):
        gl = g * group_w
        yg = y[:, gl:gl + group_w]
        ms = jnp.mean(yg * yg, axis=-1, keepdims=True)
        y_ref[:, gl:gl + group_w] = (yg * lax.rsqrt(ms + EPS)
                                     * ng_ref[:, gl:gl + group_w]).astype(BF16)


def _ssd_constants():
    ch = SSM_CHUNK
    tri = np.tril(np.ones((ch, ch), np.float32))
    ehead = np.zeros((LANES, SSM_D_INNER), np.float32)
    for h in range(SSM_HEADS):
        ehead[h, h * SSM_HEAD_DIM:(h + 1) * SSM_HEAD_DIM] = 1.0
    as_bf16 = lambda m: jnp.asarray(m, BF16)
    return as_bf16(np.tile(tri, (1, 3))), as_bf16(np.tile(ehead, (3, 1)))


def _ssd(xs, bm, cm, dt, hn, wz, a_log, d_skip_x, norm_g):
    b, t, _ = xs.shape
    rows = SSD_CHUNKS_PER_STEP * SSM_CHUNK
    tri3, ehead3 = _ssd_constants()

    def tok(width):
        return pl.BlockSpec((1, rows, width), lambda i, j: (i, j, 0))

    return pl.pallas_call(
        _ssd_kernel,
        grid=(b, t // rows),
        in_specs=[tok(SSM_D_INNER), tok(SSM_BC_DIM), tok(SSM_BC_DIM), tok(LANES), tok(D_MODEL),
                  _resident((D_MODEL, SSM_D_INNER)), _resident((1, LANES)), _resident((1, SSM_D_INNER)), _resident((1, SSM_D_INNER)),
                  _resident(tri3.shape), _resident(ehead3.shape)],
        out_specs=tok(SSM_D_INNER),
        out_shape=jax.ShapeDtypeStruct((b, t, SSM_D_INNER), BF16),
        scratch_shapes=[pltpu.VMEM((SSM_STATE, SSM_D_INNER), F32),
                        pltpu.VMEM((SSD_CHUNKS_PER_STEP, SSM_HEADS // 2, SSM_CHUNK, LANES), F32),
                        pltpu.VMEM((SSD_CHUNKS_PER_STEP, SSM_HEADS // 2, SSM_CHUNK, LANES), BF16),
                        pltpu.VMEM((SSD_CHUNKS_PER_STEP, SSM_GROUPS, SSM_CHUNK, 2 * LANES), BF16)],
        compiler_params=_params(2),
        name="ssd",
    )(xs, bm, cm, dt, hn, wz, a_log, d_skip_x, norm_g, tri3, ehead3)


def _pad_lanes(v):
    return jnp.pad(v.astype(F32), (0, LANES - v.shape[0]))[None, :]


def kernel(x, positions, ffn1_norm, ffn1_w_gate, ffn1_w_up, ffn1_w_down, mix_norm, ffn2_norm, ffn2_w_gate, ffn2_w_up, ffn2_w_down, attn_w_qkv, attn_q_norm, attn_k_norm, attn_w_o, ssm_w_in, ssm_conv_w, ssm_conv_b, ssm_dt_bias, ssm_a_log, ssm_d, ssm_norm, ssm_w_out):
    bsz, seq, d = x.shape
    rows = bsz * seq
    half = ATTN_HEAD_DIM // 2
    inv_freq = ROPE_THETA ** (-jnp.arange(half, dtype=F32) / half)
    inv_freq = jnp.concatenate([inv_freq, inv_freq])[None, :]
    pos3 = positions[:, :, None]

    def ffn(xc, layer, norm, wg, wu, wd, mixer_out=None):
        if mixer_out is not None:
            mixer_out = (mixer_out[0].reshape(rows, -1), mixer_out[1].astype(BF16))
        out = _ffn(xc.reshape(rows, d), norm[layer][None, :], wg[layer].astype(BF16),
                   wu[layer].astype(BF16), wd[layer].astype(BF16), mixer_out)
        return out.reshape(bsz, seq, d)

    ffn1 = functools.partial(ffn, norm=ffn1_norm, wg=ffn1_w_gate, wu=ffn1_w_up, wd=ffn1_w_down)
    ffn2 = functools.partial(ffn, norm=ffn2_norm, wg=ffn2_w_gate, wu=ffn2_w_up, wd=ffn2_w_down)

    x = ffn1(x, 0)
    q, k, v = _qkv(x, mix_norm[0][None, :], attn_w_qkv[0].astype(BF16), attn_q_norm[0][None, :],
                   attn_k_norm[0][None, :], pos3, inv_freq)
    x = ffn2(x, 0, mixer_out=(_moba(q, k, v), attn_w_o[0]))

    x = ffn1(x, 1)
    w_in = ssm_w_in[0]
    wz = w_in[:, :SSM_D_INNER].astype(BF16)
    wx = w_in[:, SSM_D_INNER:SSM_D_INNER + SSM_CONV_DIM].astype(BF16)
    wdt = jnp.pad(w_in[:, SSM_D_INNER + SSM_CONV_DIM:], ((0, 0), (0, LANES - SSM_HEADS))).astype(BF16)
    hn, xs, bm, cm, dt = _ssm_in(x, mix_norm[1][None, :], wx, wdt, ssm_conv_w[0],
                                 ssm_conv_b[0][None, :], _pad_lanes(ssm_dt_bias[0]))
    d_skip_x = jnp.repeat(ssm_d[0].astype(F32), SSM_HEAD_DIM)[None, :]
    y = _ssd(xs, bm, cm, dt, hn, wz, _pad_lanes(ssm_a_log[0]), d_skip_x, ssm_norm[0][None, :])
    return ffn2(x, 1, mixer_out=(y, ssm_w_out[0]))
```

```python
import functools
import math

import numpy as np
import jax
import jax.numpy as jnp
from jax import lax
from jax.experimental import pallas as pl
from jax.experimental.pallas import tpu as pltpu

F32 = jnp.float32
BF16 = jnp.bfloat16

D_MODEL = 1024
D_FF = 2816
EPS = 1e-6

ATTN_HEADS = 8
ATTN_HEAD_DIM = 128
MOBA_BLOCK = 256
MOBA_TOPK = 3
ROPE_THETA = 10000.0

SSM_D_INNER = 2048
SSM_HEAD_DIM = 64
SSM_HEADS = 32
SSM_GROUPS = 8
SSM_STATE = 128
SSM_CONV = 4
SSM_CHUNK = 128
SSM_BC_DIM = SSM_GROUPS * SSM_STATE
SSM_CONV_DIM = SSM_D_INNER + 2 * SSM_BC_DIM

LANES = 128
VMEM_LIMIT_BYTES = 56 * 1024 * 1024

FFN_ROWS = 512
FFN_COLS = 256
PROJ_ROWS = 512
SSM_IN_ROWS = 256
SSD_CHUNKS_PER_STEP = 4
SUBLANES = 8
HALO_ROWS = 16


def _resident(shape):
    nd = len(shape)
    return pl.BlockSpec(shape, lambda *_: (0,) * nd, pipeline_mode=pl.Buffered(1))


def _params(n_axes):
    return pltpu.CompilerParams(dimension_semantics=("arbitrary",) * n_axes,
                                vmem_limit_bytes=VMEM_LIMIT_BYTES)


def _rms(x, g):
    ms = jnp.mean(x * x, axis=-1, keepdims=True)
    return x * lax.rsqrt(ms + EPS) * g


def _silu(x):
    return x * jax.nn.sigmoid(x)


def _dot(a, b):
    return jnp.dot(a, b, preferred_element_type=F32)


def _dot_nt(a, b):
    return lax.dot_general(a, b, (((1,), (1,)), ((), ())), preferred_element_type=F32)


def _split2(v):
    hi = v.astype(BF16)
    return hi, (v - hi.astype(F32)).astype(BF16)


def _split3(v):
    hi = v.astype(BF16)
    r1 = v - hi.astype(F32)
    mid = r1.astype(BF16)
    lo = (r1 - mid.astype(F32)).astype(BF16)
    return hi, mid, lo


def _ffn_kernel(*refs, has_mixer_out):
    if has_mixer_out:
        x_ref, a_ref, wp_ref, g_ref, wg_ref, wu_ref, wd_ref, o_ref = refs
        x = x_ref[...] + _dot(a_ref[...], wp_ref[...])
    else:
        x_ref, g_ref, wg_ref, wu_ref, wd_ref, o_ref = refs
        x = x_ref[...]
    n = _rms(x, g_ref[...]).astype(BF16)
    acc = jnp.zeros(x.shape, F32)
    for c in range(D_FF // FFN_COLS):
        sl = slice(c * FFN_COLS, (c + 1) * FFN_COLS)
        gate = _dot(n, wg_ref[:, sl])
        up = _dot(n, wu_ref[:, sl])
        h = (_silu(gate) * up).astype(BF16)
        acc = acc + _dot(h, wd_ref[sl, :])
    o_ref[...] = x + 0.5 * acc


def _ffn(x2d, g, wg, wu, wd, mixer_out=None):
    m = x2d.shape[0]
    row = pl.BlockSpec((FFN_ROWS, D_MODEL), lambda i: (i, 0))
    weights = [_resident((1, D_MODEL)), _resident((D_MODEL, D_FF)),
               _resident((D_MODEL, D_FF)), _resident((D_FF, D_MODEL))]
    if mixer_out is None:
        operands, specs = (x2d, g, wg, wu, wd), [row] + weights
    else:
        a2d, wp = mixer_out
        k = a2d.shape[1]
        operands = (x2d, a2d, wp, g, wg, wu, wd)
        specs = [row, pl.BlockSpec((FFN_ROWS, k), lambda i: (i, 0)),
                 _resident((k, D_MODEL))] + weights
    return pl.pallas_call(
        functools.partial(_ffn_kernel, has_mixer_out=mixer_out is not None),
        grid=(m // FFN_ROWS,),
        in_specs=specs,
        out_specs=row,
        out_shape=jax.ShapeDtypeStruct(x2d.shape, F32),
        compiler_params=_params(1),
        name="ffn",
    )(*operands)


def _qkv_kernel(x_ref, g_ref, w_ref, qg_ref, kg_ref, pos_ref, invf_ref, hmean_ref,
                q_ref, k_ref, v_ref):
    n = _rms(x_ref[0], g_ref[...]).astype(BF16)
    half_dim = ATTN_HEAD_DIM // 2
    half_rows = pos_ref.shape[1] // 2
    pos = pos_ref[0].astype(F32)
    lane = lax.broadcasted_iota(jnp.int32, (half_rows, ATTN_HEAD_DIM), 1)
    low = lane < half_dim
    ang = jnp.where(low, pos[:half_rows], pos[half_rows:]) * invf_ref[...]

    def spread(t):
        swapped = pltpu.roll(t, half_dim, axis=1)
        return jnp.concatenate([jnp.where(low, t, swapped), jnp.where(low, swapped, t)], axis=0)

    cos = spread(jnp.cos(ang))
    sin = spread(jnp.sin(ang))
    lane = lax.broadcasted_iota(jnp.int32, cos.shape, 1)
    sin_signed = jnp.where(lane < half_dim, -sin, sin)

    def gain_tables(gain):
        return gain * cos, pltpu.roll(gain, half_dim, axis=1) * sin_signed

    def norm_rope_pair(t2, tables, out_ref, h0):
        ms = _dot(jnp.concatenate(_split2(t2 * t2), axis=1), hmean_ref[...])
        u2 = t2 * lax.rsqrt(ms + EPS)
        for i in range(2):
            u = u2[:, i * ATTN_HEAD_DIM:(i + 1) * ATTN_HEAD_DIM]
            out_ref[0, h0 + i] = (u * tables[0]
                                  + pltpu.roll(u, half_dim, axis=1) * tables[1]).astype(BF16)

    q_tables = gain_tables(qg_ref[...] * (ATTN_HEAD_DIM ** -0.5 * math.log2(math.e)))
    k_tables = gain_tables(kg_ref[...])
    pair = 2 * ATTN_HEAD_DIM
    pieces = [(part, h0) for h0 in range(0, ATTN_HEADS, 2) for part in range(3)]
    project = lambda part, h0: _dot(n, w_ref[:, part * D_MODEL + h0 * ATTN_HEAD_DIM:
                                             part * D_MODEL + h0 * ATTN_HEAD_DIM + pair])
    ahead = project(*pieces[0])
    for i, (part, h0) in enumerate(pieces):
        t2 = ahead
        if i + 1 < len(pieces):
            ahead = project(*pieces[i + 1])
        if part == 0:
            norm_rope_pair(t2, q_tables, q_ref, h0)
        elif part == 1:
            norm_rope_pair(t2, k_tables, k_ref, h0)
        else:
            for i_head in range(2):
                v_ref[0, h0 + i_head] = t2[:, i_head * ATTN_HEAD_DIM:
                                           (i_head + 1) * ATTN_HEAD_DIM].astype(BF16)


def _qkv(x, g, w, q_gain, k_gain, pos3, inv_freq):
    b, t, _ = x.shape
    pair = 2 * ATTN_HEAD_DIM
    same_head = (np.arange(2 * pair)[:, None] % pair) // ATTN_HEAD_DIM == np.arange(pair)[None, :] // ATTN_HEAD_DIM
    head_mean = jnp.asarray(same_head.astype(np.float32) / ATTN_HEAD_DIM, BF16)
    head_out = pl.BlockSpec((1, ATTN_HEADS, PROJ_ROWS, ATTN_HEAD_DIM), lambda i, j: (i, 0, j, 0))
    shape = jax.ShapeDtypeStruct((b, ATTN_HEADS, t, ATTN_HEAD_DIM), BF16)
    return pl.pallas_call(
        _qkv_kernel,
        grid=(b, t // PROJ_ROWS),
        in_specs=[pl.BlockSpec((1, PROJ_ROWS, D_MODEL), lambda i, j: (i, j, 0)),
                  _resident((1, D_MODEL)), _resident((D_MODEL, 3 * D_MODEL)),
                  _resident((1, ATTN_HEAD_DIM)), _resident((1, ATTN_HEAD_DIM)),
                  pl.BlockSpec((1, PROJ_ROWS, 1), lambda i, j: (i, j, 0)),
                  _resident((1, ATTN_HEAD_DIM)), _resident(head_mean.shape)],
        out_specs=[head_out, head_out, head_out],
        out_shape=[shape, shape, shape],
        compiler_params=_params(2),
        name="qkv_rope",
    )(x, g, w, q_gain, k_gain, pos3, inv_freq, head_mean)


def _moba_kernel(q_ref, k_ref, v_ref, o_ref, vt_ref, s_ref, p_ref):
    blk = MOBA_BLOCK
    t = q_ref.shape[2]
    n_blk = t // blk
    q_all = q_ref[0, 0]
    vt_ref[...] = v_ref[0, 0].T

    means = [jnp.mean(k_ref[0, 0, n * blk:(n + 1) * blk, :].astype(F32), axis=0, keepdims=True)
             for n in range(n_blk)]
    gate = _dot_nt(jnp.concatenate(means, axis=0).astype(BF16), q_all)
    key_blk = lax.broadcasted_iota(jnp.int32, gate.shape, 0)
    qry_blk = lax.broadcasted_iota(jnp.int32, gate.shape, 1) // blk
    past = key_blk < qry_blk
    gate = jnp.where(past, gate, -jnp.inf)
    rank = jnp.zeros(gate.shape, F32)
    for n in range(n_blk):
        other = gate[n:n + 1, :]
        rank = (rank + jnp.where(other > gate, 1.0, 0.0)
                + jnp.where(other == gate, jnp.where(key_blk > n, 1.0, 0.0), 0.0))
    bias = jnp.where(past & (rank < MOBA_TOPK), 0.0, -jnp.inf)

    causal = (lax.broadcasted_iota(jnp.int32, (blk, blk), 0)
              <= lax.broadcasted_iota(jnp.int32, (blk, blk), 1))

    rows = lambda n: slice(n * blk, (n + 1) * blk)

    def raw_scores(qi, n):
        s_ref[qi % 2, rows(n), :] = _dot_nt(k_ref[0, 0, rows(n), :], q_ref[0, 0, rows(qi), :])

    raw_scores(0, 0)
    for qi in range(n_blk):
        pending = list(range(qi + 2)) if qi + 1 < n_blk else []
        slot = qi % 2
        q_lo = qi * blk
        n_keys = q_lo + blk
        diag = jnp.where(causal, s_ref[slot, rows(qi), :], -jnp.inf)
        maxes = [jnp.max(s_ref[slot, rows(n), :], axis=0, keepdims=True)
                 + bias[n:n + 1, q_lo:n_keys] for n in range(qi)]
        maxes.append(jnp.max(diag, axis=0, keepdims=True))
        m = functools.reduce(jnp.maximum, maxes)
        l = jnp.zeros((1, blk), F32)
        for n in range(qi + 1):
            if n < qi:
                p = jnp.exp2(s_ref[slot, rows(n), :] + (bias[n:n + 1, q_lo:n_keys] - m))
            else:
                p = jnp.exp2(diag - m)
            l = l + jnp.sum(p, axis=0, keepdims=True)
            p_ref[slot, rows(n), :] = p.astype(BF16)
            if pending:
                raw_scores(qi + 1, pending.pop(0))
        for n in pending:
            raw_scores(qi + 1, n)
        o_t = _dot(vt_ref[:, :n_keys], p_ref[slot, :n_keys, :]) / l
        o_ref[0, q_lo:n_keys, :] = o_t.astype(BF16).T


def _moba(q, k, v):
    b, h, t, d = q.shape
    head = pl.BlockSpec((1, 1, t, d), lambda i, j: (i, j, 0, 0))
    return pl.pallas_call(
        _moba_kernel,
        grid=(b, h),
        in_specs=[head, head, head],
        out_specs=pl.BlockSpec((1, t, d), lambda i, j: (i, 0, j)),
        out_shape=jax.ShapeDtypeStruct((b, t, h * d), BF16),
        scratch_shapes=[pltpu.VMEM((d, t), BF16), pltpu.VMEM((2, t, MOBA_BLOCK), F32),
                        pltpu.VMEM((2, t, MOBA_BLOCK), BF16)],
        compiler_params=_params(2),
        name="moba_attn",
    )(q, k, v)


def _ssm_in_kernel(x_ref, halo_ref, g_ref, wx_ref, wdt_ref, cw_ref, cb_ref, dtb_ref,
                   hn_ref, xs_ref, b_ref, c_ref, dt_ref):
    j = pl.program_id(1)
    rows = x_ref.shape[1]
    g = g_ref[...]
    n = _rms(x_ref[0], g).astype(BF16)
    n_halo = _rms(halo_ref[0], g).astype(BF16)
    halo_keep = jnp.where(j > 0, 1.0, 0.0)
    sub = lax.broadcasted_iota(jnp.int32, (rows // SUBLANES, SUBLANES, SSM_CONV_DIM), 1)

    def shift_rows(cur3, prev8, back):
        rot = pltpu.roll(cur3, back, axis=1)
        rot_prev = jnp.concatenate([pltpu.roll(prev8, back, axis=0)[None], rot[:-1]], axis=0)
        return jnp.where(sub >= back, rot, rot_prev)

    hn_ref[0] = n
    prev8 = _dot(n_halo, wx_ref[...])[HALO_ROWS - SUBLANES:] * halo_keep
    cur3 = _dot(n, wx_ref[...]).reshape(sub.shape)
    w = [cw_ref[k:k + 1, :] for k in range(SSM_CONV)]
    back1 = shift_rows(cur3, prev8, 1)
    near = cur3 * w[3] + back1 * w[2]
    far = cur3 * w[1] + back1 * w[0]
    far_prev8 = prev8 * w[1] + pltpu.roll(prev8, 1, axis=0) * w[0]
    y = _silu(cb_ref[...] + near + shift_rows(far, far_prev8, 2)).reshape(rows, SSM_CONV_DIM)
    xs_ref[0] = y[:, :SSM_D_INNER]
    b_ref[0] = y[:, SSM_D_INNER:SSM_D_INNER + SSM_BC_DIM].astype(BF16)
    c_ref[0] = y[:, SSM_D_INNER + SSM_BC_DIM:].astype(BF16)

    dt_raw = _dot(n, wdt_ref[...]) + dtb_ref[...]
    softplus = jnp.maximum(dt_raw, 0.0) + jnp.log1p(jnp.exp(-jnp.abs(dt_raw)))
    lane = lax.broadcasted_iota(jnp.int32, dt_raw.shape, 1)
    dt_ref[0] = jnp.where(lane < SSM_HEADS, softplus, 0.0)


def _ssm_in(x, g, wx, wdt, conv_w, conv_b, dt_bias):
    b, t, _ = x.shape
    rows = SSM_IN_ROWS
    halo_blocks = rows // HALO_ROWS

    def tok(width, dtype):
        return (pl.BlockSpec((1, rows, width), lambda i, j: (i, j, 0)),
                jax.ShapeDtypeStruct((b, t, width), dtype))

    outs = [tok(D_MODEL, BF16), tok(SSM_D_INNER, F32), tok(SSM_BC_DIM, BF16),
            tok(SSM_BC_DIM, BF16), tok(LANES, F32)]
    return pl.pallas_call(
        _ssm_in_kernel,
        grid=(b, t // rows),
        in_specs=[pl.BlockSpec((1, rows, D_MODEL), lambda i, j: (i, j, 0)),
                  pl.BlockSpec((1, HALO_ROWS, D_MODEL),
                               lambda i, j: (i, jnp.maximum(j * halo_blocks - 1, 0), 0)),
                  _resident((1, D_MODEL)),
                  _resident((D_MODEL, SSM_CONV_DIM)), _resident((D_MODEL, LANES)),
                  _resident((SSM_CONV, SSM_CONV_DIM)), _resident((1, SSM_CONV_DIM)),
                  _resident((1, LANES))],
        out_specs=[o[0] for o in outs],
        out_shape=[o[1] for o in outs],
        compiler_params=_params(2),
        name="ssm_in",
    )(x, x, g, wx, wdt, conv_w, conv_b, dt_bias)


def _ssd_kernel(xs_ref, b_ref, c_ref, dt_ref, hn_ref, xres_ref, wz_ref, wout_ref, alog_ref, dskip_ref,
                ng_ref, tri_ref, ehead_ref, y_ref, state_ref, ybuf_ref, xdt_ref, xdd_ref):
    @pl.when(pl.program_id(1) == 0)
    def _():
        state_ref[...] = jnp.zeros(state_ref.shape, F32)

    for k in range(SSD_CHUNKS_PER_STEP):
        rows = slice(k * SSM_CHUNK, (k + 1) * SSM_CHUNK)
        _ssd_chunk(xs_ref.at[0, rows], b_ref.at[0, rows], c_ref.at[0, rows], dt_ref.at[0, rows],
                   hn_ref.at[0, rows], xres_ref.at[0, rows], wz_ref, wout_ref, alog_ref, dskip_ref,
                   ng_ref, tri_ref, ehead_ref,
                   y_ref.at[0, rows], state_ref, ybuf_ref.at[k], xdt_ref.at[k], xdd_ref.at[k])


def _ssd_chunk(xs_ref, b_ref, c_ref, dt_ref, hn_ref, xres_ref, wz_ref, wout_ref, alog_ref, dskip_ref,
               ng_ref, tri_ref, ehead_ref, y_ref, state_ref, ybuf_ref, xdt_ref, xdd_ref):
    ch = SSM_CHUNK
    heads_per_group = SSM_HEADS // SSM_GROUPS
    group_w = heads_per_group * SSM_HEAD_DIM
    hn = hn_ref[...]
    z_parts = []
    dt = dt_ref[...]
    a = dt * (-jnp.exp(alog_ref[...]) * math.log2(math.e))
    a_cs = _dot(tri_ref[...], jnp.concatenate(_split3(a), axis=0))
    cbs = [_dot_nt(c_ref[:, g * SSM_STATE:(g + 1) * SSM_STATE],
                   b_ref[:, g * SSM_STATE:(g + 1) * SSM_STATE]) for g in range(SSM_GROUPS)]
    a_last = a_cs[ch - 1:ch, :]
    decay = jnp.exp2(a_last - a_cs)
    stack = jnp.concatenate([dt, decay], axis=0)
    wide = _dot(jnp.concatenate(_split2(stack), axis=1), ehead_ref[:2 * LANES, :])
    dt_x = wide[:ch]
    decay_x = wide[ch:]
    tail = jnp.concatenate([jnp.exp2(a_last), jnp.zeros((SUBLANES - 1, LANES), F32)], axis=0)
    chunk_decay_x = _dot(jnp.concatenate(_split3(tail), axis=1), ehead_ref[...])[:1]

    acs_row = a_cs.T

    x = xs_ref[...]
    xdt = x * dt_x
    xdt_b = xdt.astype(BF16)
    xdd_b = (xdt * decay_x).astype(BF16)
    for p in range(SSM_HEADS // 2):
        xdt_ref[p] = xdt_b[:, p * LANES:(p + 1) * LANES]
    for g in range(SSM_GROUPS):
        xdd_ref[g] = xdd_b[:, g * group_w:(g + 1) * group_w]
    tril =(lax.broadcasted_iota(jnp.int32, (ch, ch), 0)
            >= lax.broadcasted_iota(jnp.int32, (ch, ch), 1))
    lane = lax.broadcasted_iota(jnp.int32, (ch, LANES), 1)

    for g in range(SSM_GROUPS):
        bg = b_ref[:, g * SSM_STATE:(g + 1) * SSM_STATE]
        cg = c_ref[:, g * SSM_STATE:(g + 1) * SSM_STATE]
        cb = cbs[g]
        bg_t = bg.T
        gl = g * group_w
        state_g = state_ref[:, gl:gl + group_w]
        lhs = []
        for r in range(heads_per_group):
            h = g * heads_per_group + r
            row = jnp.broadcast_to(acs_row[h:h + 1, :], (ch, LANES))
            col = row.T
            seg = col - row
            decay_ls = jnp.exp2(jnp.where(tril, seg, -jnp.inf))
            lhs.append(jnp.concatenate([(cb * decay_ls).astype(BF16),
                                        (cg.astype(F32) * jnp.exp2(col)).astype(BF16)], axis=1))
        for p in range(heads_per_group // 2):
            pair = g * (heads_per_group // 2) + p
            rhs = jnp.concatenate([xdt_ref[pair],
                                   state_g[:, p * LANES:(p + 1) * LANES].astype(BF16)], axis=0)
            y0 = _dot(lhs[2 * p], rhs)
            y1 = _dot(lhs[2 * p + 1], rhs)
            ybuf_ref[pair] = jnp.where(lane < SSM_HEAD_DIM, y0, y1)
        new_state = _dot(bg_t, xdd_ref[g])
        state_ref[:, gl:gl + group_w] = state_g * chunk_decay_x[:, gl:gl + group_w] + new_state
        z_parts.append(_dot(hn, wz_ref[:, gl:gl + group_w]))

    y_ssd = jnp.concatenate([ybuf_ref[p] for p in range(SSM_HEADS // 2)], axis=1)
    z = jnp.concatenate(z_parts, axis=1)
    y = (y_ssd + dskip_ref[...] * x) * _silu(z)
    out = xres_ref[...]
    for g in range(SSM_GROUPS):
        gl = g * group_w
        yg = y[:, gl:gl + group_w]
        ms = jnp.mean(yg * yg, axis=-1, keepdims=True)
        yn = (yg * lax.rsqrt(ms + EPS) * ng_ref[:, gl:gl + group_w]).astype(BF16)
        out = out + _dot(yn, wout_ref[gl:gl + group_w, :])
    y_ref[...] = out


def _ssd_constants():
    ch = SSM_CHUNK
    tri = np.tril(np.ones((ch, ch), np.float32))
    ehead = np.zeros((LANES, SSM_D_INNER), np.float32)
    for h in range(SSM_HEADS):
        ehead[h, h * SSM_HEAD_DIM:(h + 1) * SSM_HEAD_DIM] = 1.0
    as_bf16 = lambda m: jnp.asarray(m, BF16)
    return as_bf16(np.tile(tri, (1, 3))), as_bf16(np.tile(ehead, (3, 1)))


def _ssd(xs, bm, cm, dt, hn, x_res, wz, w_out, a_log, d_skip_x, norm_g):
    b, t, _ = xs.shape
    rows = SSD_CHUNKS_PER_STEP * SSM_CHUNK
    tri3, ehead3 = _ssd_constants()

    def tok(width):
        return pl.BlockSpec((1, rows, width), lambda i, j: (i, j, 0))

    return pl.pallas_call(
        _ssd_kernel,
        grid=(b, t // rows),
        in_specs=[tok(SSM_D_INNER), tok(SSM_BC_DIM), tok(SSM_BC_DIM), tok(LANES), tok(D_MODEL),
                  tok(D_MODEL), _resident((D_MODEL, SSM_D_INNER)),
                  _resident((SSM_D_INNER, D_MODEL)), _resident((1, LANES)), _resident((1, SSM_D_INNER)), _resident((1, SSM_D_INNER)),
                  _resident(tri3.shape), _resident(ehead3.shape)],
        out_specs=tok(D_MODEL),
        out_shape=jax.ShapeDtypeStruct((b, t, D_MODEL), F32),
        scratch_shapes=[pltpu.VMEM((SSM_STATE, SSM_D_INNER), F32),
                        pltpu.VMEM((SSD_CHUNKS_PER_STEP, SSM_HEADS // 2, SSM_CHUNK, LANES), F32),
                        pltpu.VMEM((SSD_CHUNKS_PER_STEP, SSM_HEADS // 2, SSM_CHUNK, LANES), BF16),
                        pltpu.VMEM((SSD_CHUNKS_PER_STEP, SSM_GROUPS, SSM_CHUNK, 2 * LANES), BF16)],
        compiler_params=_params(2),
        name="ssd",
    )(xs, bm, cm, dt, hn, x_res, wz, w_out, a_log, d_skip_x, norm_g, tri3, ehead3)


def _pad_lanes(v):
    return jnp.pad(v.astype(F32), (0, LANES - v.shape[0]))[None, :]


def kernel(x, positions, ffn1_norm, ffn1_w_gate, ffn1_w_up, ffn1_w_down, mix_norm, ffn2_norm, ffn2_w_gate, ffn2_w_up, ffn2_w_down, attn_w_qkv, attn_q_norm, attn_k_norm, attn_w_o, ssm_w_in, ssm_conv_w, ssm_conv_b, ssm_dt_bias, ssm_a_log, ssm_d, ssm_norm, ssm_w_out):
    bsz, seq, d = x.shape
    rows = bsz * seq
    half = ATTN_HEAD_DIM // 2
    inv_freq = ROPE_THETA ** (-jnp.arange(half, dtype=F32) / half)
    inv_freq = jnp.concatenate([inv_freq, inv_freq])[None, :]
    pos3 = positions[:, :, None]

    def ffn(xc, layer, norm, wg, wu, wd, mixer_out=None):
        if mixer_out is not None:
            mixer_out = (mixer_out[0].reshape(rows, -1), mixer_out[1].astype(BF16))
        out = _ffn(xc.reshape(rows, d), norm[layer][None, :], wg[layer].astype(BF16),
                   wu[layer].astype(BF16), wd[layer].astype(BF16), mixer_out)
        return out.reshape(bsz, seq, d)

    ffn1 = functools.partial(ffn, norm=ffn1_norm, wg=ffn1_w_gate, wu=ffn1_w_up, wd=ffn1_w_down)
    ffn2 = functools.partial(ffn, norm=ffn2_norm, wg=ffn2_w_gate, wu=ffn2_w_up, wd=ffn2_w_down)

    x = ffn1(x, 0)
    q, k, v = _qkv(x, mix_norm[0][None, :], attn_w_qkv[0].astype(BF16), attn_q_norm[0][None, :],
                   attn_k_norm[0][None, :], pos3, inv_freq)
    x = ffn2(x, 0, mixer_out=(_moba(q, k, v), attn_w_o[0]))

    x = ffn1(x, 1)
    w_in = ssm_w_in[0]
    wz = w_in[:, :SSM_D_INNER].astype(BF16)
    wx = w_in[:, SSM_D_INNER:SSM_D_INNER + SSM_CONV_DIM].astype(BF16)
    wdt = jnp.pad(w_in[:, SSM_D_INNER + SSM_CONV_DIM:], ((0, 0), (0, LANES - SSM_HEADS))).astype(BF16)
    hn, xs, bm, cm, dt = _ssm_in(x, mix_norm[1][None, :], wx, wdt, ssm_conv_w[0],
                                 ssm_conv_b[0][None, :], _pad_lanes(ssm_dt_bias[0]))
    d_skip_x = jnp.repeat(ssm_d[0].astype(F32), SSM_HEAD_DIM)[None, :]
    x = _ssd(xs, bm, cm, dt, hn, x, wz, ssm_w_out[0].astype(BF16), _pad_lanes(ssm_a_log[0]), d_skip_x,
             ssm_norm[0][None, :])
    return ffn2(x, 1)
```
